```python
import math
import jax, jax.numpy as jnp
from jax import lax
import numpy as np

D_MODEL = 1024
BATCH = 8
SEQ = 4096
DEPTH = 2

CHUNK = 64
D_FF = 2816
MACARON_WEIGHT = 0.5
RMS_EPS = 1e-6
A_WIDTH = D_MODEL // 2
A_GROUPS = 8
CONV_WIDTH = 3
SB_HEADS = 8
SB_HEAD_DIM = (D_MODEL // 2) // SB_HEADS
B_WIDTH = SB_HEADS * SB_HEAD_DIM
SB_BLOCK = 128
AB_IN_COLS = 3 * A_WIDTH + 3 * B_WIDTH
C_WIDTH = D_MODEL
C_HEAD_DIM = 128
C_HEADS = C_WIDTH // C_HEAD_DIM
C_IN_COLS = 4 * C_WIDTH
N_EVEN = (DEPTH + 1) // 2
N_ODD = DEPTH // 2

kernel_name = "hybrid_shortconv_stickbreak_hgrn2_macaron"


def rms_norm(x, gain):
    x32 = x.astype(jnp.float32)
    y = x32 * lax.rsqrt(jnp.mean(x32 * x32, axis=-1, keepdims=True) + RMS_EPS)
    return (y * gain.astype(jnp.float32)).astype(x.dtype)


def swiglu(h, w_gate, w_up, w_down):
    return (jax.nn.silu(h @ w_gate) * (h @ w_up)) @ w_down


def stick_breaking_attention(q, k, v):
    bsz, nh, s_len, dh = q.shape
    nb = s_len // SB_BLOCK
    scale = 1.0 / math.sqrt(dh)
    qb = jnp.moveaxis(q.reshape(bsz, nh, nb, SB_BLOCK, dh), 2, 0)
    starts = jnp.arange(nb, dtype=jnp.int32) * SB_BLOCK
    kpos = jnp.arange(s_len, dtype=jnp.int32)

    def one_block(args):
        qi, start = args
        qpos = start + jnp.arange(SB_BLOCK, dtype=jnp.int32)
        mask = kpos[None, :] < qpos[:, None]
        z = jnp.einsum('bhqd,bhkd->bhqk', qi, k) * scale
        log_beta = jax.nn.log_sigmoid(z)
        log_keep = jnp.where(mask, jax.nn.log_sigmoid(-z), 0.0)
        later = lax.cumsum(log_keep, axis=3, reverse=True) - log_keep
        w = jnp.where(mask, jnp.exp(log_beta + later), 0.0)
        return jnp.einsum('bhqk,bhkd->bhqd', w, v)

    out = lax.map(one_block, (qb, starts))
    return jnp.moveaxis(out, 0, 2).reshape(bsz, nh, s_len, dh)


def shortconv_stickbreak_mixer(h, w_in, conv_w, w_out):
    bsz, s_len, _ = h.shape
    proj = h @ w_in
    a_b, a_c, a_x, q, k, v = jnp.split(proj, 6, axis=-1)
    u = a_c * a_x
    conv = lax.conv_general_dilated(
        u, conv_w[:, None, :].astype(u.dtype), window_strides=(1,),
        padding=[(CONV_WIDTH - 1, 0)], dimension_numbers=('NWC', 'WIO', 'NWC'),
        feature_group_count=A_WIDTH)
    y_a = a_b * conv
    def heads(t):
        return t.reshape(bsz, s_len, SB_HEADS, SB_HEAD_DIM).transpose(0, 2, 1, 3).astype(jnp.float32)
    y_b = stick_breaking_attention(heads(q), heads(k), heads(v))
    y_b = y_b.transpose(0, 2, 1, 3).reshape(bsz, s_len, B_WIDTH).astype(h.dtype)
    return jnp.concatenate([y_a, y_b], axis=-1) @ w_out


def chunkwise_gated_recurrence(q, log_f, k, v):
    bsz, nh, s_len, dk = q.shape
    dv = v.shape[-1]
    n_chunks = s_len // CHUNK

    def to_chunks(t):
        return jnp.moveaxis(t.reshape(bsz, nh, n_chunks, CHUNK, t.shape[-1]), 2, 0)

    tri = jnp.tril(jnp.ones((CHUNK, CHUNK), dtype=bool))

    def step(state, inp):
        qc, gc, kc, vc = inp
        b = jnp.cumsum(gc, axis=2)
        o_inter = jnp.einsum('bhtk,bhkv->bhtv', qc * jnp.exp(b), state)
        diff = b[:, :, :, None, :] - b[:, :, None, :, :]
        decay = jnp.exp(jnp.where(tri[None, None, :, :, None], diff, -jnp.inf))
        scores = jnp.einsum('bhtk,bhsk,bhtsk->bhts', qc, kc, decay)
        o_intra = jnp.einsum('bhts,bhsv->bhtv', scores, vc)
        b_last = b[:, :, -1:, :]
        new_state = (jnp.exp(b_last[:, :, 0, :])[..., None] * state
                     + jnp.einsum('bhsk,bhsv->bhkv', kc * jnp.exp(b_last - b), vc))
        return new_state, o_inter + o_intra

    state0 = jnp.zeros((bsz, nh, dk, dv), jnp.float32)
    _, out = lax.scan(step, state0, (to_chunks(q), to_chunks(log_f), to_chunks(k), to_chunks(v)))
    return jnp.moveaxis(out, 0, 2).reshape(bsz, nh, s_len, dv)


def hgrn2_mixer(h, w_in, lower_bound, out_norm, w_out):
    bsz, s_len, _ = h.shape
    proj = h @ w_in
    q, f, i, g = jnp.split(proj, 4, axis=-1)
    lb = lower_bound.astype(jnp.float32)
    log_f = jnp.logaddexp(jnp.log(lb), jnp.log1p(-lb) + jax.nn.log_sigmoid(f.astype(jnp.float32)))
    k = -jnp.expm1(log_f)
    q = jax.nn.silu(q.astype(jnp.float32))

    def heads(t):
        return t.reshape(bsz, s_len, C_HEADS, C_HEAD_DIM).transpose(0, 2, 1, 3).astype(jnp.float32)

    o = chunkwise_gated_recurrence(heads(q), heads(log_f), heads(k), heads(i))
    o = o * lax.rsqrt(jnp.mean(o * o, axis=-1, keepdims=True) + RMS_EPS) * out_norm.astype(jnp.float32)
    o = o.transpose(0, 2, 1, 3).reshape(bsz, s_len, C_WIDTH)
    o = (o * jax.nn.silu(g.astype(jnp.float32))).astype(h.dtype)
    return o @ w_out


def setup_inputs(seed: int = 0) -> dict:
    key = jax.random.key(seed)
    ks = jax.random.split(key, 20)
    f32 = jnp.float32

    def w(k, shape, fan_in):
        return jax.random.normal(k, shape, f32) * (fan_in ** -0.5)

    def gain(k, shape):
        return 1.0 + 0.02 * jax.random.normal(k, shape, f32)

    return {
        "x": jax.random.normal(ks[0], (BATCH, SEQ, D_MODEL), f32),
        "ffn_pre_norm": gain(ks[1], (DEPTH, D_MODEL)),
        "ffn_pre_w_gate": w(ks[2], (DEPTH, D_MODEL, D_FF), D_MODEL),
        "ffn_pre_w_up": w(ks[3], (DEPTH, D_MODEL, D_FF), D_MODEL),
        "ffn_pre_w_down": w(ks[4], (DEPTH, D_FF, D_MODEL), D_FF),
        "mix_norm": gain(ks[5], (DEPTH, D_MODEL)),
        "ffn_post_norm": gain(ks[6], (DEPTH, D_MODEL)),
        "ffn_post_w_gate": w(ks[7], (DEPTH, D_MODEL, D_FF), D_MODEL),
        "ffn_post_w_up": w(ks[8], (DEPTH, D_MODEL, D_FF), D_MODEL),
        "ffn_post_w_down": w(ks[9], (DEPTH, D_FF, D_MODEL), D_FF),
        "ab_w_in": w(ks[10], (N_EVEN, D_MODEL, AB_IN_COLS), D_MODEL),
        "ab_conv_w": w(ks[11], (N_EVEN, CONV_WIDTH, A_WIDTH), CONV_WIDTH),
        "ab_w_out": w(ks[12], (N_EVEN, A_WIDTH + B_WIDTH, D_MODEL), A_WIDTH + B_WIDTH),
        "c_w_in": w(ks[13], (N_ODD, D_MODEL, C_IN_COLS), D_MODEL),
        "c_lower_bounds": 0.1 * jax.random.normal(ks[14], (DEPTH, C_WIDTH), f32),
        "c_out_norm": gain(ks[15], (N_ODD, C_HEAD_DIM)),
        "c_w_out": w(ks[16], (N_ODD, C_WIDTH, D_MODEL), C_WIDTH),
        "final_norm": gain(ks[17], (D_MODEL,)),
    }


def reference(x, ffn_pre_norm, ffn_pre_w_gate, ffn_pre_w_up, ffn_pre_w_down, mix_norm,
              ffn_post_norm, ffn_post_w_gate, ffn_post_w_up, ffn_post_w_down,
              ab_w_in, ab_conv_w, ab_w_out, c_w_in, c_lower_bounds, c_out_norm, c_w_out,
              final_norm):
    lb_soft = jax.nn.softmax(c_lower_bounds.astype(jnp.float32), axis=0)
    lb_cum = jnp.cumsum(lb_soft, axis=0)
    lower_bounds = lb_cum - lb_cum[0:1]

    h = x
    for layer in range(DEPTH):
        h = h + MACARON_WEIGHT * swiglu(rms_norm(h, ffn_pre_norm[layer]), ffn_pre_w_gate[layer],
                                        ffn_pre_w_up[layer], ffn_pre_w_down[layer])
        hn = rms_norm(h, mix_norm[layer])
        if layer % 2 == 0:
            e = layer // 2
            h = h + shortconv_stickbreak_mixer(hn, ab_w_in[e], ab_conv_w[e], ab_w_out[e])
        else:
            o = layer // 2
            h = h + hgrn2_mixer(hn, c_w_in[o], lower_bounds[layer], c_out_norm[o], c_w_out[o])
        h = h + MACARON_WEIGHT * swiglu(rms_norm(h, ffn_post_norm[layer]), ffn_post_w_gate[layer],
                                        ffn_post_w_up[layer], ffn_post_w_down[layer])
    return rms_norm(h, final_norm)
```

```python
import functools
import math

import numpy as np
import jax
import jax.numpy as jnp
from jax import lax
from jax.experimental import pallas as pl
from jax.experimental.pallas import tpu as pltpu

RMS_EPS = 1e-6
MACARON_WEIGHT = 0.5
LANES = 128
VMEM_LIMIT_BYTES = 56 * 1024 * 1024
SB_EXIT_LOGDECAY = 104.0

F32 = jnp.float32
BF16 = jnp.bfloat16


def _cparams(sem):
    return pltpu.CompilerParams(dimension_semantics=sem, vmem_limit_bytes=VMEM_LIMIT_BYTES)


def _rms_norm_f32(x, gain):
    return x * lax.rsqrt(jnp.mean(x * x, axis=-1, keepdims=True) + RMS_EPS) * gain


def _sigmoid(x):
    return 1.0 / (1.0 + jnp.exp(-x))


def _ffn_kernel(h_ref, gain_ref, wg_ref, wu_ref, wd_ref, fgain_ref, o_ref, xn_ref, acc_ref,
                *, final_norm):
    j = pl.program_id(1)

    @pl.when(j == 0)
    def _():
        xn_ref[...] = _rms_norm_f32(h_ref[...], gain_ref[...]).astype(BF16)
        acc_ref[...] = jnp.zeros_like(acc_ref)

    xn = xn_ref[...]
    g = jnp.dot(xn, wg_ref[...], preferred_element_type=F32)
    u = jnp.dot(xn, wu_ref[...], preferred_element_type=F32)
    a = (g * _sigmoid(g) * u).astype(BF16)
    acc_ref[...] += jnp.dot(a, wd_ref[...], preferred_element_type=F32)

    @pl.when(j == pl.num_programs(1) - 1)
    def _():
        out = h_ref[...] + MACARON_WEIGHT * acc_ref[...]
        if final_norm:
            out = _rms_norm_f32(out, fgain_ref[...])
        o_ref[...] = out


def ffn(h, gain, wg, wu, wd, fgain, *, final_norm, tm, tf):
    n, d = h.shape
    f = wg.shape[1]
    grid = (n // tm, f // tf)
    return pl.pallas_call(
        functools.partial(_ffn_kernel, final_norm=final_norm),
        grid=grid,
        in_specs=[
            pl.BlockSpec((tm, d), lambda i, j: (i, 0)),
            pl.BlockSpec((1, d), lambda i, j: (0, 0)),
            pl.BlockSpec((d, tf), lambda i, j: (0, j)),
            pl.BlockSpec((d, tf), lambda i, j: (0, j)),
            pl.BlockSpec((tf, d), lambda i, j: (j, 0)),
            pl.BlockSpec((1, d), lambda i, j: (0, 0)),
        ],
        out_specs=pl.BlockSpec((tm, d), lambda i, j: (i, 0)),
        out_shape=jax.ShapeDtypeStruct((n, d), F32),
        scratch_shapes=[pltpu.VMEM((tm, d), BF16), pltpu.VMEM((tm, d), F32)],
        compiler_params=_cparams(("parallel", "arbitrary")),
        name="ffn",
    )(h, gain, wg, wu, wd, fgain)


def _norm_proj_kernel(h_ref, gain_ref, w_ref, o_ref, xn_ref):
    @pl.when(pl.program_id(1) == 0)
    def _():
        xn_ref[...] = _rms_norm_f32(h_ref[...], gain_ref[...]).astype(BF16)

    o_ref[...] = jnp.dot(xn_ref[...], w_ref[...], preferred_element_type=F32).astype(o_ref.dtype)


def norm_proj(h, gain, w, *, tm, tn):
    n, d = h.shape
    c = w.shape[1]
    return pl.pallas_call(
        _norm_proj_kernel,
        grid=(n // tm, c // tn),
        in_specs=[
            pl.BlockSpec((tm, d), lambda i, j: (i, 0)),
            pl.BlockSpec((1, d), lambda i, j: (0, 0)),
            pl.BlockSpec((d, tn), lambda i, j: (0, j)),
        ],
        out_specs=pl.BlockSpec((tm, tn), lambda i, j: (i, j)),
        out_shape=jax.ShapeDtypeStruct((n, c), BF16),
        scratch_shapes=[pltpu.VMEM((tm, d), BF16)],
        compiler_params=_cparams(("parallel", "arbitrary")),
        name="norm_proj",
    )(h, gain, w)


def _out_proj_kernel(y_ref, w_ref, h_ref, o_ref):
    o_ref[...] = h_ref[...] + jnp.dot(y_ref[...], w_ref[...], preferred_element_type=F32)


def out_proj(y, w, h, *, tm):
    n, d = h.shape
    c = y.shape[1]
    return pl.pallas_call(
        _out_proj_kernel,
        grid=(n // tm,),
        in_specs=[
            pl.BlockSpec((tm, c), lambda i: (i, 0)),
            pl.BlockSpec((c, d), lambda i: (0, 0)),
            pl.BlockSpec((tm, d), lambda i: (i, 0)),
        ],
        out_specs=pl.BlockSpec((tm, d), lambda i: (i, 0)),
        out_shape=jax.ShapeDtypeStruct((n, d), F32),
        compiler_params=_cparams(("parallel",)),
        name="out_proj",
    )(y, w, h)


def _sb_attn_kernel(q_ref, k_ref, v_ref, tri_ref, o_ref, acc_ref, dec_ref, *, tq, dh, scale):
    i = pl.program_id(2)
    lane = lax.broadcasted_iota(jnp.int32, (tq, LANES), 1)
    head_b = lane >= dh
    q = q_ref[0] * jnp.asarray(scale, BF16)
    zero = jnp.zeros_like(q)
    q_heads = (jnp.where(head_b, zero, q), jnp.where(head_b, q, zero))
    tri = tri_ref[...]
    row = lax.broadcasted_iota(jnp.int32, (tq, tq), 0)
    col = lax.broadcasted_iota(jnp.int32, (tq, tq), 1)
    causal = col < row

    acc_ref[...] = jnp.zeros_like(acc_ref)
    dec_ref[...] = jnp.zeros_like(dec_ref)

    def block(j, diagonal):
        start = pl.multiple_of(j * tq, tq)
        kb = k_ref[0, pl.ds(start, tq), :]
        vb = v_ref[0, pl.ds(start, tq), :]
        outs = []
        for hh in range(2):
            z = lax.dot_general(q_heads[hh], kb, (((1,), (1,)), ((), ())),
                                preferred_element_type=F32)
            sp = jnp.maximum(z, 0.0) + jnp.log1p(jnp.exp(-jnp.abs(z)))
            log_beta = z - sp
            if diagonal:
                sp = jnp.where(causal, sp, 0.0)
            sp_hi = sp.astype(BF16)
            sp_lo = (sp - sp_hi.astype(F32)).astype(BF16)
            later = (jnp.dot(sp_hi, tri, preferred_element_type=F32)
                     + jnp.dot(sp_lo, tri, preferred_element_type=F32))
            dec = dec_ref[hh]
            w = jnp.exp(log_beta - later - dec)
            if diagonal:
                w = jnp.where(causal, w, 0.0)
            outs.append(jnp.dot(w.astype(BF16), vb, preferred_element_type=F32))
            dec_ref[hh] = dec + later[:, 0:1] + sp[:, 0:1]
        acc_ref[...] += jnp.where(head_b, outs[1], outs[0])

    def undecayed():
        return jnp.min(dec_ref[...]) < SB_EXIT_LOGDECAY

    block(i, True)

    def cond(state):
        j, go = state
        return jnp.logical_and(j >= 0, go)

    def body(state):
        j, _ = state
        block(j, False)
        return j - 1, undecayed()

    lax.while_loop(cond, body, (i - 1, undecayed()))
    o_ref[0] = acc_ref[...].astype(o_ref.dtype)


def sb_attention(proj, *, n_heads, dh, q_col, k_col, v_col, tq):
    bsz, s_len, _ = proj.shape
    assert 2 * dh == LANES and n_heads % 2 == 0
    assert q_col % LANES == 0 and k_col % LANES == 0 and v_col % LANES == 0
    pairs = n_heads // 2
    qb, kb, vb = q_col // LANES, k_col // LANES, v_col // LANES
    tri = jnp.asarray(np.tril(np.ones((tq, tq), np.float32), -1), BF16)
    return pl.pallas_call(
        functools.partial(_sb_attn_kernel, tq=tq, dh=dh, scale=1.0 / math.sqrt(dh)),
        grid=(bsz, pairs, s_len // tq),
        in_specs=[
            pl.BlockSpec((1, tq, LANES), lambda b, p, i: (b, i, qb + p)),
            pl.BlockSpec((1, s_len, LANES), lambda b, p, i: (b, 0, kb + p)),
            pl.BlockSpec((1, s_len, LANES), lambda b, p, i: (b, 0, vb + p)),
            pl.BlockSpec((tq, tq), lambda b, p, i: (0, 0)),
        ],
        out_specs=pl.BlockSpec((1, tq, LANES), lambda b, p, i: (b, i, p)),
        out_shape=jax.ShapeDtypeStruct((bsz, s_len, pairs * LANES), BF16),
        scratch_shapes=[pltpu.VMEM((tq, LANES), F32), pltpu.VMEM((2, tq, 1), F32)],
        compiler_params=_cparams(("parallel", "parallel", "arbitrary")),
        name="sb_attn",
    )(proj, proj, proj, tri)


def _conv_out_kernel(ab_ref, ac_ref, ax_ref, pc_ref, px_ref, yb_ref, cw_ref, wa_ref, wb_ref, h_ref,
                     o_ref, *, tm):
    i = pl.program_id(1)
    u = ac_ref[0].astype(F32) * ax_ref[0].astype(F32)
    prev = pc_ref[0].astype(F32) * px_ref[0].astype(F32)
    prev = jnp.where(i == 0, 0.0, prev)
    row = lax.broadcasted_iota(jnp.int32, u.shape, 0)
    u1 = pltpu.roll(u, 1, 0)
    u1 = jnp.where(row == 0, prev[7:8, :], u1)
    u2 = pltpu.roll(u, 2, 0)
    u2 = jnp.where(row == 0, prev[6:7, :], jnp.where(row == 1, prev[7:8, :], u2))
    cw = cw_ref[...]
    conv = cw[0:1, :] * u2 + cw[1:2, :] * u1 + cw[2:3, :] * u
    y_a = (ab_ref[0].astype(F32) * conv).astype(BF16)
    o_ref[0] = (h_ref[0]
                + jnp.dot(y_a, wa_ref[...], preferred_element_type=F32)
                + jnp.dot(yb_ref[0], wb_ref[...], preferred_element_type=F32))


def conv_out(proj, y_b, conv_w, w_out_a, w_out_b, h, *, a_width, tm):
    bsz, s_len, d = h.shape
    halo = tm // 8
    return pl.pallas_call(
        functools.partial(_conv_out_kernel, tm=tm),
        grid=(bsz, s_len // tm),
        in_specs=[
            pl.BlockSpec((1, tm, a_width), lambda b, i: (b, i, 0)),
            pl.BlockSpec((1, tm, a_width), lambda b, i: (b, i, 1)),
            pl.BlockSpec((1, tm, a_width), lambda b, i: (b, i, 2)),
            pl.BlockSpec((1, 8, a_width), lambda b, i: (b, jnp.maximum(i * halo - 1, 0), 1)),
            pl.BlockSpec((1, 8, a_width), lambda b, i: (b, jnp.maximum(i * halo - 1, 0), 2)),
            pl.BlockSpec((1, tm, y_b.shape[-1]), lambda b, i: (b, i, 0)),
            pl.BlockSpec(conv_w.shape, lambda b, i: (0, 0)),
            pl.BlockSpec(w_out_a.shape, lambda b, i: (0, 0)),
            pl.BlockSpec(w_out_b.shape, lambda b, i: (0, 0)),
            pl.BlockSpec((1, tm, d), lambda b, i: (b, i, 0)),
        ],
        out_specs=pl.BlockSpec((1, tm, d), lambda b, i: (b, i, 0)),
        out_shape=jax.ShapeDtypeStruct((bsz, s_len, d), F32),
        compiler_params=_cparams(("parallel", "arbitrary")),
        name="conv_out",
    )(proj, proj, proj, proj, proj, y_b, conv_w, w_out_a, w_out_b, h)


def _level_ids(c):
    t = np.arange(c)[:, None]
    s = np.arange(c)[None, :]
    x = np.bitwise_xor(t, s)
    n_levels = int(math.log2(c))
    msb = np.floor(np.log2(np.maximum(x, 1))).astype(np.int64)
    lvl = (n_levels - 1) - msb
    lvl = np.where(x == 0, n_levels, lvl)
    lvl = np.where(s > t, -1, lvl)
    return lvl.astype(np.int32)


def _hgrn2_kernel(q_ref, f_ref, i_ref, g_ref, lb_ref, on_ref, lvl_ref, tri_ref, o_ref,
                  state_ref, b_ref, *, c, n_heads, dk):
    n_levels = int(math.log2(c))

    @pl.when(pl.program_id(1) == 0)
    def _():
        state_ref[...] = jnp.zeros_like(state_ref)

    lvl = lvl_ref[...]
    tri = tri_ref[...]
    row = lax.broadcasted_iota(jnp.int32, (c, dk), 0)
    nt = (((1,), (1,)), ((), ()))
    tn = (((0,), (0,)), ((), ()))

    def head(hd, carry):
        col = pl.multiple_of(hd * dk, dk)
        q_raw = q_ref[0, :, pl.ds(col, dk)].astype(F32)
        f_raw = f_ref[0, :, pl.ds(col, dk)].astype(F32)
        v = i_ref[0, :, pl.ds(col, dk)]
        g_raw = g_ref[0, :, pl.ds(col, dk)].astype(F32)
        lb = lb_ref[:, pl.ds(col, dk)]

        e = jnp.exp(-jnp.abs(f_raw))
        s = 1.0 / (1.0 + e)
        es = e * s
        pos = f_raw >= 0
        sig_f = jnp.where(pos, s, es)
        sig_nf = jnp.where(pos, es, s)
        fgate = lb + (1.0 - lb) * sig_f
        k = (1.0 - lb) * sig_nf
        x = jnp.log(fgate)
        q = q_raw * _sigmoid(q_raw)

        x_hi = x.astype(BF16)
        r1 = x - x_hi.astype(F32)
        x_mid = r1.astype(BF16)
        x_lo = (r1 - x_mid.astype(F32)).astype(BF16)
        x3 = jnp.concatenate([x_hi, x_mid, x_lo], axis=1)
        b3 = jnp.dot(tri, x3, preferred_element_type=F32)
        b = b3[:, :dk] + b3[:, dk:2 * dk] + b3[:, 2 * dk:]
        b_ref[...] = b

        scores = jnp.where(lvl == n_levels,
                           lax.dot_general(q.astype(BF16), k.astype(BF16), nt,
                                           preferred_element_type=F32), 0.0)
        for level in range(n_levels):
            blk = c >> level
            half = blk // 2
            if blk >= 8:
                refs = [jnp.broadcast_to(b_ref[pl.ds(m * blk + half - 1, 1), :], (blk, dk))
                        for m in range(c // blk)]
                ref = refs[0] if len(refs) == 1 else jnp.concatenate(refs, axis=0)
                gl = jnp.exp(-jnp.abs(b - ref))
            elif blk == 4:
                p = row & 3
                f_next = pltpu.roll(fgate, c - 1, 0)
                f_prev = pltpu.roll(fgate, 1, 0)
                gl = jnp.where(p == 0, f_next,
                               jnp.where(p == 1, 1.0,
                                         jnp.where(p == 2, fgate, fgate * f_prev)))
            else:
                gl = jnp.where((row & 1) == 1, fgate, 1.0)
            ql = (q * gl).astype(BF16)
            kl = (k * gl).astype(BF16)
            sl = lax.dot_general(ql, kl, nt, preferred_element_type=F32)
            scores = jnp.where(lvl == level, sl, scores)

        o_intra = jnp.dot(scores.astype(BF16), v, preferred_element_type=F32)

        state = state_ref[hd]
        decay_in = jnp.exp(b)
        o_inter = jnp.dot((q * decay_in).astype(BF16), state.astype(BF16),
                          preferred_element_type=F32)
        b_last = b_ref[pl.ds(c - 1, 1), :]
        k_out = (k * jnp.exp(b_last - b)).astype(BF16)
        kv = lax.dot_general(k_out, v, tn, preferred_element_type=F32)
        eye = (lax.broadcasted_iota(jnp.int32, (dk, dk), 0)
               == lax.broadcasted_iota(jnp.int32, (dk, dk), 1))
        scale_col = jnp.sum(jnp.where(eye, jnp.exp(b_last), 0.0), axis=1, keepdims=True)
        state_ref[hd] = state * scale_col + kv

        o = o_inter + o_intra
        o = o * lax.rsqrt(jnp.mean(o * o, axis=-1, keepdims=True) + RMS_EPS) * on_ref[...]
        o_ref[0, :, pl.ds(col, dk)] = (o * (g_raw * _sigmoid(g_raw))).astype(o_ref.dtype)
        return carry

    lax.fori_loop(0, n_heads, head, 0)


def hgrn2(proj, lower_bound, out_norm, *, n_heads, dk, c):
    bsz, s_len, _ = proj.shape
    w = n_heads * dk
    assert dk == LANES and s_len % c == 0 and (c & (c - 1)) == 0 and c >= 8
    lvl = jnp.asarray(_level_ids(c))
    tri = jnp.asarray(np.tril(np.ones((c, c), np.float32)), BF16)
    return pl.pallas_call(
        functools.partial(_hgrn2_kernel, c=c, n_heads=n_heads, dk=dk),
        grid=(bsz, s_len // c),
        in_specs=[
            pl.BlockSpec((1, c, w), lambda b, i: (b, i, 0)),
            pl.BlockSpec((1, c, w), lambda b, i: (b, i, 1)),
            pl.BlockSpec((1, c, w), lambda b, i: (b, i, 2)),
            pl.BlockSpec((1, c, w), lambda b, i: (b, i, 3)),
            pl.BlockSpec((1, w), lambda b, i: (0, 0)),
            pl.BlockSpec((1, dk), lambda b, i: (0, 0)),
            pl.BlockSpec((c, c), lambda b, i: (0, 0)),
            pl.BlockSpec((c, c), lambda b, i: (0, 0)),
        ],
        out_specs=pl.BlockSpec((1, c, w), lambda b, i: (b, i, 0)),
        out_shape=jax.ShapeDtypeStruct((bsz, s_len, w), BF16),
        scratch_shapes=[pltpu.VMEM((n_heads, dk, dk), F32), pltpu.VMEM((c, dk), F32)],
        compiler_params=_cparams(("parallel", "arbitrary")),
        name="hgrn2",
    )(proj, proj, proj, proj, lower_bound, out_norm, lvl, tri)


def _tiles(n_tokens, d_ff):
    tm = min(1024, n_tokens)
    tf = 256 if d_ff % 256 == 0 else d_ff
    return tm, tf


def trunk(x, ffn_pre_norm, ffn_pre_w_gate, ffn_pre_w_up, ffn_pre_w_down, mix_norm,
          ffn_post_norm, ffn_post_w_gate, ffn_post_w_up, ffn_post_w_down,
          ab_w_in, ab_conv_w, ab_w_out, c_w_in, c_lower_bounds, c_out_norm, c_w_out,
          final_norm, *, sb_heads, sb_head_dim, c_heads, c_head_dim, sb_tq, c_chunk):
    bsz, s_len, d = x.shape
    depth = ffn_pre_norm.shape[0]
    n = bsz * s_len
    d_ff = ffn_pre_w_gate.shape[-1]
    tm, tf = _tiles(n, d_ff)
    tm_seq = min(tm, s_len)
    bf = lambda t: t.astype(BF16)
    row = lambda t: t.reshape(1, -1).astype(F32)

    lb_soft = jax.nn.softmax(c_lower_bounds.astype(F32), axis=0)
    lb_cum = jnp.cumsum(lb_soft, axis=0)
    lower_bounds = lb_cum - lb_cum[0:1]

    a_width = ab_w_in.shape[-1] // 6
    h = x.reshape(n, d)
    for layer in range(depth):
        h = ffn(h, row(ffn_pre_norm[layer]), bf(ffn_pre_w_gate[layer]), bf(ffn_pre_w_up[layer]),
                bf(ffn_pre_w_down[layer]), row(final_norm), final_norm=False, tm=tm, tf=tf)
        if layer % 2 == 0:
            e = layer // 2
            proj = norm_proj(h, row(mix_norm[layer]), bf(ab_w_in[e]), tm=tm, tn=512)
            proj = proj.reshape(bsz, s_len, -1)
            y_b = sb_attention(proj, n_heads=sb_heads, dh=sb_head_dim, q_col=3 * a_width,
                               k_col=3 * a_width + sb_heads * sb_head_dim,
                               v_col=3 * a_width + 2 * sb_heads * sb_head_dim, tq=sb_tq)
            w_out = bf(ab_w_out[e])
            h = conv_out(proj, y_b, ab_conv_w[e].astype(F32), w_out[:a_width], w_out[a_width:],
                         h.reshape(bsz, s_len, d), a_width=a_width, tm=tm_seq).reshape(n, d)
        else:
            o = layer // 2
            proj = norm_proj(h, row(mix_norm[layer]), bf(c_w_in[o]), tm=tm, tn=512)
            proj = proj.reshape(bsz, s_len, -1)
            y = hgrn2(proj, row(lower_bounds[layer]), row(c_out_norm[o]),
                      n_heads=c_heads, dk=c_head_dim, c=c_chunk)
            h = out_proj(y.reshape(n, -1), bf(c_w_out[o]), h, tm=tm)
        h = ffn(h, row(ffn_post_norm[layer]), bf(ffn_post_w_gate[layer]), bf(ffn_post_w_up[layer]),
                bf(ffn_post_w_down[layer]), row(final_norm),
                final_norm=(layer == depth - 1), tm=tm, tf=tf)
    return h.reshape(bsz, s_len, d)


def kernel(x, ffn_pre_norm, ffn_pre_w_gate, ffn_pre_w_up, ffn_pre_w_down, mix_norm, ffn_post_norm,
           ffn_post_w_gate, ffn_post_w_up, ffn_post_w_down, ab_w_in, ab_conv_w, ab_w_out, c_w_in,
           c_lower_bounds, c_out_norm, c_w_out, final_norm):
    return trunk(x, ffn_pre_norm, ffn_pre_w_gate, ffn_pre_w_up, ffn_pre_w_down, mix_norm,
                 ffn_post_norm, ffn_post_w_gate, ffn_post_w_up, ffn_post_w_down,
                 ab_w_in, ab_conv_w, ab_w_out, c_w_in, c_lower_bounds, c_out_norm, c_w_out,
                 final_norm, sb_heads=8, sb_head_dim=64, c_heads=8, c_head_dim=128,
                 sb_tq=256, c_chunk=256)
```

```python
import functools
import math

import numpy as np
import jax
import jax.numpy as jnp
from jax import lax
from jax.experimental import pallas as pl
from jax.experimental.pallas import tpu as pltpu

RMS_EPS = 1e-6
MACARON_WEIGHT = 0.5
LANES = 128
VMEM_LIMIT_BYTES = 56 * 1024 * 1024
SB_EXIT_LOGDECAY = 104.0

F32 = jnp.float32
BF16 = jnp.bfloat16


def _cparams(sem):
    return pltpu.CompilerParams(dimension_semantics=sem, vmem_limit_bytes=VMEM_LIMIT_BYTES)


def _rms_norm_f32(x, gain):
    return x * lax.rsqrt(jnp.mean(x * x, axis=-1, keepdims=True) + RMS_EPS) * gain


def _sigmoid(x):
    return 1.0 / (1.0 + jnp.exp(-x))


def _ffn_kernel(h_ref, gain_ref, wg_ref, wu_ref, wd_ref, fgain_ref, o_ref, xn_ref, acc_ref,
                *, final_norm, tf):
    xn_ref[...] = _rms_norm_f32(h_ref[...], gain_ref[...]).astype(BF16)
    acc_ref[...] = jnp.zeros_like(acc_ref)

    def chunk(j, carry):
        col = pl.multiple_of(j * tf, tf)
        xn = xn_ref[...]
        g = jnp.dot(xn, wg_ref[:, pl.ds(col, tf)], preferred_element_type=F32)
        u = jnp.dot(xn, wu_ref[:, pl.ds(col, tf)], preferred_element_type=F32)
        a = (g * _sigmoid(g) * u).astype(BF16)
        acc_ref[...] += jnp.dot(a, wd_ref[pl.ds(col, tf), :], preferred_element_type=F32)
        return carry

    lax.fori_loop(0, wg_ref.shape[1] // tf, chunk, 0)
    out = h_ref[...] + MACARON_WEIGHT * acc_ref[...]
    if final_norm:
        out = _rms_norm_f32(out, fgain_ref[...])
    o_ref[...] = out


def ffn(h, gain, wg, wu, wd, fgain, *, final_norm, tm, tf):
    n, d = h.shape
    f = wg.shape[1]
    resident = lambda shape: pl.BlockSpec(shape, lambda i: (0, 0), pipeline_mode=pl.Buffered(1))
    return pl.pallas_call(
        functools.partial(_ffn_kernel, final_norm=final_norm, tf=tf),
        grid=(n // tm,),
        in_specs=[
            pl.BlockSpec((tm, d), lambda i: (i, 0)),
            resident((1, d)),
            resident((d, f)),
            resident((d, f)),
            resident((f, d)),
            resident((1, d)),
        ],
        out_specs=pl.BlockSpec((tm, d), lambda i: (i, 0)),
        out_shape=jax.ShapeDtypeStruct((n, d), F32),
        scratch_shapes=[pltpu.VMEM((tm, d), BF16), pltpu.VMEM((tm, d), F32)],
        compiler_params=_cparams(("parallel",)),
        name="ffn",
    )(h, gain, wg, wu, wd, fgain)


def _norm_proj_kernel(h_ref, gain_ref, w_ref, o_ref, xn_ref):
    @pl.when(pl.program_id(1) == 0)
    def _():
        xn_ref[...] = _rms_norm_f32(h_ref[...], gain_ref[...]).astype(BF16)

    o_ref[...] = jnp.dot(xn_ref[...], w_ref[...], preferred_element_type=F32).astype(o_ref.dtype)


def norm_proj(h, gain, w, *, tm, tn):
    n, d = h.shape
    c = w.shape[1]
    return pl.pallas_call(
        _norm_proj_kernel,
        grid=(n // tm, c // tn),
        in_specs=[
            pl.BlockSpec((tm, d), lambda i, j: (i, 0)),
            pl.BlockSpec((1, d), lambda i, j: (0, 0)),
            pl.BlockSpec((d, tn), lambda i, j: (0, j)),
        ],
        out_specs=pl.BlockSpec((tm, tn), lambda i, j: (i, j)),
        out_shape=jax.ShapeDtypeStruct((n, c), BF16),
        scratch_shapes=[pltpu.VMEM((tm, d), BF16)],
        compiler_params=_cparams(("parallel", "arbitrary")),
        name="norm_proj",
    )(h, gain, w)


def _out_proj_kernel(y_ref, w_ref, h_ref, o_ref):
    o_ref[...] = h_ref[...] + jnp.dot(y_ref[...], w_ref[...], preferred_element_type=F32)


def out_proj(y, w, h, *, tm):
    n, d = h.shape
    c = y.shape[1]
    return pl.pallas_call(
        _out_proj_kernel,
        grid=(n // tm,),
        in_specs=[
            pl.BlockSpec((tm, c), lambda i: (i, 0)),
            pl.BlockSpec((c, d), lambda i: (0, 0)),
            pl.BlockSpec((tm, d), lambda i: (i, 0)),
        ],
        out_specs=pl.BlockSpec((tm, d), lambda i: (i, 0)),
        out_shape=jax.ShapeDtypeStruct((n, d), F32),
        compiler_params=_cparams(("parallel",)),
        name="out_proj",
    )(y, w, h)


def _sb_attn_kernel(q_ref, k_ref, v_ref, tri_ref, o_ref, acc_ref, dec_ref, *, tq, dh, scale, n_q):
    lane = lax.broadcasted_iota(jnp.int32, (tq, LANES), 1)
    head_b = lane >= dh
    tri = tri_ref[...]
    row = lax.broadcasted_iota(jnp.int32, (tq, tq), 0)
    col = lax.broadcasted_iota(jnp.int32, (tq, tq), 1)
    causal = col < row
    nt = (((1,), (1,)), ((), ()))

    def key_block(j):
        start = pl.multiple_of(j * tq, tq)
        return k_ref[0, pl.ds(start, tq), :], v_ref[0, pl.ds(start, tq), :]

    def head_block(qh, kb, vb, dec, diagonal):
        z = lax.dot_general(qh, kb, nt, preferred_element_type=F32)
        sp = jnp.maximum(z, 0.0) + jnp.log(1.0 + jnp.exp(-jnp.abs(z)))
        log_beta = z - sp
        if diagonal:
            sp = jnp.where(causal, sp, 0.0)
        later = jnp.dot(sp.astype(BF16), tri, preferred_element_type=F32)
        arg = log_beta - later
        total = later[:, 0:1] + sp[:, 0:1]
        if dec is not None:
            arg = arg - dec
            total = total + dec
        w = jnp.exp(arg)
        if diagonal:
            w = jnp.where(causal, w, 0.0)
        return jnp.dot(w.astype(BF16), vb, preferred_element_type=F32), total

    def q_block(i, n_static):
        qs = pl.multiple_of(i * tq, tq)
        q = q_ref[0, pl.ds(qs, tq), :] * jnp.asarray(scale, BF16)
        zero = jnp.zeros_like(q)
        q_heads = (jnp.where(head_b, zero, q), jnp.where(head_b, q, zero))
        blocks = [key_block(i - d) for d in range(n_static)]
        outs = []
        for hh in range(2):
            out, dec = head_block(q_heads[hh], *blocks[0], None, True)
            for kb, vb in blocks[1:]:
                more, dec = head_block(q_heads[hh], kb, vb, dec, False)
                out = out + more
            outs.append(out)
            dec_ref[hh] = dec
        acc_ref[...] = jnp.where(head_b, outs[1], outs[0])

        def undecayed():
            return jnp.min(dec_ref[...]) < SB_EXIT_LOGDECAY

        def cond(state):
            j, go = state
            return jnp.logical_and(j >= 0, go)

        def body(state):
            j, _ = state
            kb, vb = key_block(j)
            more = []
            for hh in range(2):
                out, dec = head_block(q_heads[hh], kb, vb, dec_ref[hh], False)
                dec_ref[hh] = dec
                more.append(out)
            acc_ref[...] += jnp.where(head_b, more[1], more[0])
            return j - 1, undecayed()

        lax.while_loop(cond, body, (i - n_static, undecayed()))
        o_ref[0, pl.ds(qs, tq), :] = acc_ref[...].astype(o_ref.dtype)

    q_block(jnp.int32(0), 1)

    def rest(i, carry):
        q_block(i, 2)
        return carry

    lax.fori_loop(1, n_q, rest, 0)


def sb_attention(proj, *, n_heads, dh, q_col, k_col, v_col, tq):
    bsz, s_len, _ = proj.shape
    assert 2 * dh == LANES and n_heads % 2 == 0
    assert q_col % LANES == 0 and k_col % LANES == 0 and v_col % LANES == 0
    pairs = n_heads // 2
    qb, kb, vb = q_col // LANES, k_col // LANES, v_col // LANES
    tri = jnp.asarray(np.tril(np.ones((tq, tq), np.float32), -1), BF16)
    seq = lambda c: pl.BlockSpec((1, s_len, LANES), lambda b, p: (b, 0, c + p))
    return pl.pallas_call(
        functools.partial(_sb_attn_kernel, tq=tq, dh=dh, scale=1.0 / math.sqrt(dh),
                          n_q=s_len // tq),
        grid=(bsz, pairs),
        in_specs=[seq(qb), seq(kb), seq(vb), pl.BlockSpec((tq, tq), lambda b, p: (0, 0))],
        out_specs=seq(0),
        out_shape=jax.ShapeDtypeStruct((bsz, s_len, pairs * LANES), BF16),
        scratch_shapes=[pltpu.VMEM((tq, LANES), F32), pltpu.VMEM((2, tq, 1), F32)],
        compiler_params=_cparams(("parallel", "parallel")),
        name="sb_attn",
    )(proj, proj, proj, tri)


def _conv_out_kernel(ab_ref, ac_ref, ax_ref, pc_ref, px_ref, yb_ref, cw_ref, wa_ref, wb_ref, h_ref,
                     o_ref, *, tm):
    i = pl.program_id(1)
    u = ac_ref[0].astype(F32) * ax_ref[0].astype(F32)
    prev = pc_ref[0].astype(F32) * px_ref[0].astype(F32)
    prev = jnp.where(i == 0, 0.0, prev)
    row = lax.broadcasted_iota(jnp.int32, u.shape, 0)
    u1 = pltpu.roll(u, 1, 0)
    u1 = jnp.where(row == 0, prev[7:8, :], u1)
    u2 = pltpu.roll(u, 2, 0)
    u2 = jnp.where(row == 0, prev[6:7, :], jnp.where(row == 1, prev[7:8, :], u2))
    cw = cw_ref[...]
    conv = cw[0:1, :] * u2 + cw[1:2, :] * u1 + cw[2:3, :] * u
    y_a = (ab_ref[0].astype(F32) * conv).astype(BF16)
    o_ref[0] = (h_ref[0]
                + jnp.dot(y_a, wa_ref[...], preferred_element_type=F32)
                + jnp.dot(yb_ref[0], wb_ref[...], preferred_element_type=F32))


def conv_out(proj, y_b, conv_w, w_out_a, w_out_b, h, *, a_width, tm):
    bsz, s_len, d = h.shape
    halo = tm // 8
    return pl.pallas_call(
        functools.partial(_conv_out_kernel, tm=tm),
        grid=(bsz, s_len // tm),
        in_specs=[
            pl.BlockSpec((1, tm, a_width), lambda b, i: (b, i, 0)),
            pl.BlockSpec((1, tm, a_width), lambda b, i: (b, i, 1)),
            pl.BlockSpec((1, tm, a_width), lambda b, i: (b, i, 2)),
            pl.BlockSpec((1, 8, a_width), lambda b, i: (b, jnp.maximum(i * halo - 1, 0), 1)),
            pl.BlockSpec((1, 8, a_width), lambda b, i: (b, jnp.maximum(i * halo - 1, 0), 2)),
            pl.BlockSpec((1, tm, y_b.shape[-1]), lambda b, i: (b, i, 0)),
            pl.BlockSpec(conv_w.shape, lambda b, i: (0, 0)),
            pl.BlockSpec(w_out_a.shape, lambda b, i: (0, 0)),
            pl.BlockSpec(w_out_b.shape, lambda b, i: (0, 0)),
            pl.BlockSpec((1, tm, d), lambda b, i: (b, i, 0)),
        ],
        out_specs=pl.BlockSpec((1, tm, d), lambda b, i: (b, i, 0)),
        out_shape=jax.ShapeDtypeStruct((bsz, s_len, d), F32),
        compiler_params=_cparams(("parallel", "arbitrary")),
        name="conv_out",
    )(proj, proj, proj, proj, proj, y_b, conv_w, w_out_a, w_out_b, h)


def _level_ids(c):
    t = np.arange(c)[:, None]
    s = np.arange(c)[None, :]
    x = np.bitwise_xor(t, s)
    n_levels = int(math.log2(c))
    msb = np.floor(np.log2(np.maximum(x, 1))).astype(np.int64)
    lvl = (n_levels - 1) - msb
    lvl = np.where(x == 0, n_levels, lvl)
    lvl = np.where(s > t, -1, lvl)
    return lvl.astype(np.int32)


def _hgrn2_kernel(q_ref, f_ref, i_ref, g_ref, lb_ref, on_ref, lvl_ref, tri_ref, o_ref,
                  state_ref, b_ref, *, c, n_heads, dk):
    n_levels = int(math.log2(c))

    @pl.when(pl.program_id(1) == 0)
    def _():
        state_ref[...] = jnp.zeros_like(state_ref)

    lvl = lvl_ref[...]
    tri = tri_ref[...]
    row = lax.broadcasted_iota(jnp.int32, (c, dk), 0)
    nt = (((1,), (1,)), ((), ()))
    tn = (((0,), (0,)), ((), ()))

    def head(hd, carry):
        col = pl.multiple_of(hd * dk, dk)
        q_raw = q_ref[0, :, pl.ds(col, dk)].astype(F32)
        f_raw = f_ref[0, :, pl.ds(col, dk)].astype(F32)
        v = i_ref[0, :, pl.ds(col, dk)]
        g_raw = g_ref[0, :, pl.ds(col, dk)].astype(F32)
        lb = lb_ref[:, pl.ds(col, dk)]

        e = jnp.exp(-jnp.abs(f_raw))
        s = 1.0 / (1.0 + e)
        es = e * s
        pos = f_raw >= 0
        sig_f = jnp.where(pos, s, es)
        sig_nf = jnp.where(pos, es, s)
        fgate = lb + (1.0 - lb) * sig_f
        k = (1.0 - lb) * sig_nf
        x = jnp.log2(fgate)
        q = q_raw * _sigmoid(q_raw)

        x_hi = x.astype(BF16)
        r1 = x - x_hi.astype(F32)
        x_mid = r1.astype(BF16)
        x_lo = (r1 - x_mid.astype(F32)).astype(BF16)
        x3 = jnp.concatenate([x_hi, x_mid, x_lo], axis=1)
        b3 = jnp.dot(tri, x3, preferred_element_type=F32)
        b = b3[:, :dk] + b3[:, dk:2 * dk] + b3[:, 2 * dk:]
        b_ref[...] = b

        q_bf = q.astype(BF16)
        k_bf = k.astype(BF16)
        diag = lax.dot_general(q_bf, k_bf, nt, preferred_element_type=F32).astype(BF16)
        scores = jnp.where(lvl == n_levels, diag, jnp.zeros_like(diag))
        for level in range(n_levels):
            blk = c >> level
            half = blk // 2
            if blk >= 8:
                refs = [jnp.broadcast_to(b_ref[pl.ds(m * blk + half - 1, 1), :], (blk, dk))
                        for m in range(c // blk)]
                ref = refs[0] if len(refs) == 1 else jnp.concatenate(refs, axis=0)
                gl = jnp.exp2(-jnp.abs(b - ref))
            elif blk == 4:
                p = row & 3
                f_next = pltpu.roll(fgate, c - 1, 0)
                f_prev = pltpu.roll(fgate, 1, 0)
                gl = jnp.where(p == 0, f_next,
                               jnp.where(p == 1, 1.0,
                                         jnp.where(p == 2, fgate, fgate * f_prev)))
            else:
                gl = jnp.where((row & 1) == 1, fgate, 1.0)
            gl = gl.astype(BF16)
            sl = lax.dot_general(q_bf * gl, k_bf * gl, nt, preferred_element_type=F32)
            scores = jnp.where(lvl == level, sl.astype(BF16), scores)

        o_intra = jnp.dot(scores, v, preferred_element_type=F32)

        state = state_ref[hd]
        decay_in = jnp.exp2(b)
        o_inter = jnp.dot((q * decay_in).astype(BF16), state.astype(BF16),
                          preferred_element_type=F32)
        b_last = b_ref[pl.ds(c - 1, 1), :]
        k_out = (k * jnp.exp2(b_last - b)).astype(BF16)
        kv = lax.dot_general(k_out, v, tn, preferred_element_type=F32)
        eye = (lax.broadcasted_iota(jnp.int32, (dk, dk), 0)
               == lax.broadcasted_iota(jnp.int32, (dk, dk), 1))
        scale_col = jnp.sum(jnp.where(eye, jnp.exp2(b_last), 0.0), axis=1, keepdims=True)
        state_ref[hd] = state * scale_col + kv

        o = o_inter + o_intra
        o = o * lax.rsqrt(jnp.mean(o * o, axis=-1, keepdims=True) + RMS_EPS) * on_ref[...]
        o_ref[0, :, pl.ds(col, dk)] = (o * (g_raw * _sigmoid(g_raw))).astype(o_ref.dtype)
        return carry

    lax.fori_loop(0, n_heads, head, 0)


def hgrn2(proj, lower_bound, out_norm, *, n_heads, dk, c):
    bsz, s_len, _ = proj.shape
    w = n_heads * dk
    assert dk == LANES and s_len % c == 0 and (c & (c - 1)) == 0 and c >= 8
    lvl = jnp.asarray(_level_ids(c), BF16)
    tri = jnp.asarray(np.tril(np.ones((c, c), np.float32)), BF16)
    return pl.pallas_call(
        functools.partial(_hgrn2_kernel, c=c, n_heads=n_heads, dk=dk),
        grid=(bsz, s_len // c),
        in_specs=[
            pl.BlockSpec((1, c, w), lambda b, i: (b, i, 0)),
            pl.BlockSpec((1, c, w), lambda b, i: (b, i, 1)),
            pl.BlockSpec((1, c, w), lambda b, i: (b, i, 2)),
            pl.BlockSpec((1, c, w), lambda b, i: (b, i, 3)),
            pl.BlockSpec((1, w), lambda b, i: (0, 0)),
            pl.BlockSpec((1, dk), lambda b, i: (0, 0)),
            pl.BlockSpec((c, c), lambda b, i: (0, 0)),
            pl.BlockSpec((c, c), lambda b, i: (0, 0)),
        ],
        out_specs=pl.BlockSpec((1, c, w), lambda b, i: (b, i, 0)),
        out_shape=jax.ShapeDtypeStruct((bsz, s_len, w), BF16),
        scratch_shapes=[pltpu.VMEM((n_heads, dk, dk), F32), pltpu.VMEM((c, dk), F32)],
        compiler_params=_cparams(("parallel", "arbitrary")),
        name="hgrn2",
    )(proj, proj, proj, proj, lower_bound, out_norm, lvl, tri)


def _tiles(n_tokens, d_ff):
    tm = min(1024, n_tokens)
    tf = 256 if d_ff % 256 == 0 else d_ff
    return tm, tf


def trunk(x, ffn_pre_norm, ffn_pre_w_gate, ffn_pre_w_up, ffn_pre_w_down, mix_norm,
          ffn_post_norm, ffn_post_w_gate, ffn_post_w_up, ffn_post_w_down,
          ab_w_in, ab_conv_w, ab_w_out, c_w_in, c_lower_bounds, c_out_norm, c_w_out,
          final_norm, *, sb_heads, sb_head_dim, c_heads, c_head_dim, sb_tq, c_chunk):
    bsz, s_len, d = x.shape
    depth = ffn_pre_norm.shape[0]
    n = bsz * s_len
    d_ff = ffn_pre_w_gate.shape[-1]
    tm, tf = _tiles(n, d_ff)
    tm_seq = min(tm, s_len)
    bf = lambda t: t.astype(BF16)
    row = lambda t: t.reshape(1, -1).astype(F32)

    lb_soft = jax.nn.softmax(c_lower_bounds.astype(F32), axis=0)
    lb_cum = jnp.cumsum(lb_soft, axis=0)
    lower_bounds = lb_cum - lb_cum[0:1]

    a_width = ab_w_in.shape[-1] // 6
    h = x.reshape(n, d)
    for layer in range(depth):
        h = ffn(h, row(ffn_pre_norm[layer]), bf(ffn_pre_w_gate[layer]), bf(ffn_pre_w_up[layer]),
                bf(ffn_pre_w_down[layer]), row(final_norm), final_norm=False, tm=tm, tf=tf)
        if layer % 2 == 0:
            e = layer // 2
            proj = norm_proj(h, row(mix_norm[layer]), bf(ab_w_in[e]), tm=tm, tn=512)
            proj = proj.reshape(bsz, s_len, -1)
            y_b = sb_attention(proj, n_heads=sb_heads, dh=sb_head_dim, q_col=3 * a_width,
                               k_col=3 * a_width + sb_heads * sb_head_dim,
                               v_col=3 * a_width + 2 * sb_heads * sb_head_dim, tq=sb_tq)
            w_out = bf(ab_w_out[e])
            h = conv_out(proj, y_b, ab_conv_w[e].astype(F32), w_out[:a_width], w_out[a_width:],
                         h.reshape(bsz, s_len, d), a_width=a_width, tm=tm_seq).reshape(n, d)
        else:
            o = layer // 2
            proj = norm_proj(h, row(mix_norm[layer]), bf(c_w_in[o]), tm=tm, tn=512)
            proj = proj.reshape(bsz, s_len, -1)
            y = hgrn2(proj, row(lower_bounds[layer]), row(c_out_norm[o]),
                      n_heads=c_heads, dk=c_head_dim, c=c_chunk)
            h = out_proj(y.reshape(n, -1), bf(c_w_out[o]), h, tm=tm)
        h = ffn(h, row(ffn_post_norm[layer]), bf(ffn_post_w_gate[layer]), bf(ffn_post_w_up[layer]),
                bf(ffn_post_w_down[layer]), row(final_norm),
                final_norm=(layer == depth - 1), tm=tm, tf=tf)
    return h.reshape(bsz, s_len, d)


def kernel(x, ffn_pre_norm, ffn_pre_w_gate, ffn_pre_w_up, ffn_pre_w_down, mix_norm, ffn_post_norm,
           ffn_post_w_gate, ffn_post_w_up, ffn_post_w_down, ab_w_in, ab_conv_w, ab_w_out, c_w_in,
           c_lower_bounds, c_out_norm, c_w_out, final_norm):
    return trunk(x, ffn_pre_norm, ffn_pre_w_gate, ffn_pre_w_up, ffn_pre_w_down, mix_norm,
                 ffn_post_norm, ffn_post_w_gate, ffn_post_w_up, ffn_post_w_down,
                 ab_w_in, ab_conv_w, ab_w_out, c_w_in, c_lower_bounds, c_out_norm, c_w_out,
                 final_norm, sb_heads=8, sb_head_dim=64, c_heads=8, c_head_dim=128,
                 sb_tq=256, c_chunk=256)
```

```python
import functools
import math

import numpy as np
import jax
import jax.numpy as jnp
from jax import lax
from jax.experimental import pallas as pl
from jax.experimental.pallas import tpu as pltpu

RMS_EPS = 1e-6
MACARON_WEIGHT = 0.5
LANES = 128
VMEM_LIMIT_BYTES = 56 * 1024 * 1024
SB_EXIT_LOGDECAY = 104.0

F32 = jnp.float32
BF16 = jnp.bfloat16


def _cparams(sem):
    return pltpu.CompilerParams(dimension_semantics=sem, vmem_limit_bytes=VMEM_LIMIT_BYTES)


def _rms_norm_f32(x, gain):
    return x * lax.rsqrt(jnp.mean(x * x, axis=-1, keepdims=True) + RMS_EPS) * gain


def _sigmoid(x):
    return 1.0 / (1.0 + jnp.exp(-x))


def _ffn_kernel(h_ref, gain_ref, wg_ref, wu_ref, wd_ref, fgain_ref, o_ref, xn_ref, acc_ref,
                *, final_norm, tf):
    xn_ref[...] = _rms_norm_f32(h_ref[...], gain_ref[...]).astype(BF16)
    acc_ref[...] = jnp.zeros_like(acc_ref)

    def chunk(j, carry):
        col = pl.multiple_of(j * tf, tf)
        xn = xn_ref[...]
        g = jnp.dot(xn, wg_ref[:, pl.ds(col, tf)], preferred_element_type=F32)
        u = jnp.dot(xn, wu_ref[:, pl.ds(col, tf)], preferred_element_type=F32)
        a = (g * _sigmoid(g) * u).astype(BF16)
        acc_ref[...] += jnp.dot(a, wd_ref[pl.ds(col, tf), :], preferred_element_type=F32)
        return carry

    lax.fori_loop(0, wg_ref.shape[1] // tf, chunk, 0, unroll=True)
    out = h_ref[...] + MACARON_WEIGHT * acc_ref[...]
    if final_norm:
        out = _rms_norm_f32(out, fgain_ref[...])
    o_ref[...] = out


def ffn(h, gain, wg, wu, wd, fgain, *, final_norm, tm, tf):
    n, d = h.shape
    f = wg.shape[1]
    resident = lambda shape: pl.BlockSpec(shape, lambda i: (0, 0), pipeline_mode=pl.Buffered(1))
    return pl.pallas_call(
        functools.partial(_ffn_kernel, final_norm=final_norm, tf=tf),
        grid=(n // tm,),
        in_specs=[
            pl.BlockSpec((tm, d), lambda i: (i, 0)),
            resident((1, d)),
            resident((d, f)),
            resident((d, f)),
            resident((f, d)),
            resident((1, d)),
        ],
        out_specs=pl.BlockSpec((tm, d), lambda i: (i, 0)),
        out_shape=jax.ShapeDtypeStruct((n, d), F32),
        scratch_shapes=[pltpu.VMEM((tm, d), BF16), pltpu.VMEM((tm, d), F32)],
        compiler_params=_cparams(("parallel",)),
        name="ffn",
    )(h, gain, wg, wu, wd, fgain)


def _norm_proj_kernel(h_ref, gain_ref, w_ref, o_ref, xn_ref, *, tn):
    xn_ref[...] = _rms_norm_f32(h_ref[...], gain_ref[...]).astype(BF16)

    def chunk(j, carry):
        col = pl.multiple_of(j * tn, tn)
        o_ref[:, pl.ds(col, tn)] = jnp.dot(xn_ref[...], w_ref[:, pl.ds(col, tn)],
                                           preferred_element_type=F32).astype(o_ref.dtype)
        return carry

    lax.fori_loop(0, w_ref.shape[1] // tn, chunk, 0, unroll=True)


def norm_proj(h, gain, w, *, tm, tn):
    n, d = h.shape
    c = w.shape[1]
    resident = lambda shape: pl.BlockSpec(shape, lambda i: (0, 0), pipeline_mode=pl.Buffered(1))
    return pl.pallas_call(
        functools.partial(_norm_proj_kernel, tn=tn),
        grid=(n // tm,),
        in_specs=[pl.BlockSpec((tm, d), lambda i: (i, 0)), resident((1, d)), resident((d, c))],
        out_specs=pl.BlockSpec((tm, c), lambda i: (i, 0)),
        out_shape=jax.ShapeDtypeStruct((n, c), BF16),
        scratch_shapes=[pltpu.VMEM((tm, d), BF16)],
        compiler_params=_cparams(("parallel",)),
        name="norm_proj",
    )(h, gain, w)


def _out_proj_kernel(y_ref, w_ref, h_ref, o_ref):
    o_ref[...] = h_ref[...] + jnp.dot(y_ref[...], w_ref[...], preferred_element_type=F32)


def out_proj(y, w, h, *, tm):
    n, d = h.shape
    c = y.shape[1]
    return pl.pallas_call(
        _out_proj_kernel,
        grid=(n // tm,),
        in_specs=[
            pl.BlockSpec((tm, c), lambda i: (i, 0)),
            pl.BlockSpec((c, d), lambda i: (0, 0)),
            pl.BlockSpec((tm, d), lambda i: (i, 0)),
        ],
        out_specs=pl.BlockSpec((tm, d), lambda i: (i, 0)),
        out_shape=jax.ShapeDtypeStruct((n, d), F32),
        compiler_params=_cparams(("parallel",)),
        name="out_proj",
    )(y, w, h)


def _sb_attn_kernel(q_ref, k_ref, v_ref, tri_ref, o_ref, acc_ref, dec_ref, *, tq, dh, scale, n_q):
    lane = lax.broadcasted_iota(jnp.int32, (tq, LANES), 1)
    head_b = lane >= dh
    tri = tri_ref[...]
    row = lax.broadcasted_iota(jnp.int32, (tq, tq), 0)
    col = lax.broadcasted_iota(jnp.int32, (tq, tq), 1)
    causal = col < row
    nt = (((1,), (1,)), ((), ()))

    def key_block(j):
        start = pl.multiple_of(j * tq, tq)
        return k_ref[0, pl.ds(start, tq), :], v_ref[0, pl.ds(start, tq), :]

    def head_block(qh, kb, vb, dec, diagonal):
        z = lax.dot_general(qh, kb, nt, preferred_element_type=F32)
        sp = jnp.maximum(z, 0.0) + jnp.log(1.0 + jnp.exp(-jnp.abs(z)))
        log_beta = z - sp
        if diagonal:
            sp = jnp.where(causal, sp, 0.0)
        later = jnp.dot(sp.astype(BF16), tri, preferred_element_type=F32)
        arg = log_beta - later
        total = later[:, 0:1] + sp[:, 0:1]
        if dec is not None:
            arg = arg - dec
            total = total + dec
        w = jnp.exp(arg)
        if diagonal:
            w = jnp.where(causal, w, 0.0)
        return jnp.dot(w.astype(BF16), vb, preferred_element_type=F32), total

    def q_block(i, n_static):
        qs = pl.multiple_of(i * tq, tq)
        q = q_ref[0, pl.ds(qs, tq), :] * jnp.asarray(scale, BF16)
        zero = jnp.zeros_like(q)
        q_heads = (jnp.where(head_b, zero, q), jnp.where(head_b, q, zero))
        blocks = [key_block(i - d) for d in range(n_static)]
        outs = []
        for hh in range(2):
            out, dec = head_block(q_heads[hh], *blocks[0], None, True)
            for kb, vb in blocks[1:]:
                more, dec = head_block(q_heads[hh], kb, vb, dec, False)
                out = out + more
            outs.append(out)
            dec_ref[hh] = dec
        acc_ref[...] = jnp.where(head_b, outs[1], outs[0])

        def undecayed():
            return jnp.min(dec_ref[...]) < SB_EXIT_LOGDECAY

        def cond(state):
            j, go = state
            return jnp.logical_and(j >= 0, go)

        def body(state):
            j, _ = state
            kb, vb = key_block(j)
            more = []
            for hh in range(2):
                out, dec = head_block(q_heads[hh], kb, vb, dec_ref[hh], False)
                dec_ref[hh] = dec
                more.append(out)
            acc_ref[...] += jnp.where(head_b, more[1], more[0])
            return j - 1, undecayed()

        lax.while_loop(cond, body, (i - n_static, undecayed()))
        o_ref[0, pl.ds(qs, tq), :] = acc_ref[...].astype(o_ref.dtype)

    q_block(jnp.int32(0), 1)

    def rest(i, carry):
        q_block(i, 2)
        return carry

    lax.fori_loop(1, n_q, rest, 0)


def sb_attention(proj, *, n_heads, dh, q_col, k_col, v_col, tq):
    bsz, s_len, _ = proj.shape
    assert 2 * dh == LANES and n_heads % 2 == 0
    assert q_col % LANES == 0 and k_col % LANES == 0 and v_col % LANES == 0
    pairs = n_heads // 2
    qb, kb, vb = q_col // LANES, k_col // LANES, v_col // LANES
    tri = jnp.asarray(np.tril(np.ones((tq, tq), np.float32), -1), BF16)
    seq = lambda c: pl.BlockSpec((1, s_len, LANES), lambda b, p: (b, 0, c + p))
    return pl.pallas_call(
        functools.partial(_sb_attn_kernel, tq=tq, dh=dh, scale=1.0 / math.sqrt(dh),
                          n_q=s_len // tq),
        grid=(bsz, pairs),
        in_specs=[seq(qb), seq(kb), seq(vb), pl.BlockSpec((tq, tq), lambda b, p: (0, 0))],
        out_specs=seq(0),
        out_shape=jax.ShapeDtypeStruct((bsz, s_len, pairs * LANES), BF16),
        scratch_shapes=[pltpu.VMEM((tq, LANES), F32), pltpu.VMEM((2, tq, 1), F32)],
        compiler_params=_cparams(("parallel", "parallel")),
        name="sb_attn",
    )(proj, proj, proj, tri)


def _conv_out_kernel(ab_ref, ac_ref, ax_ref, pc_ref, px_ref, yb_ref, cw_ref, wa_ref, wb_ref, h_ref,
                     o_ref, *, tm):
    i = pl.program_id(1)
    u = ac_ref[0].astype(F32) * ax_ref[0].astype(F32)
    prev = pc_ref[0].astype(F32) * px_ref[0].astype(F32)
    prev = jnp.where(i == 0, 0.0, prev)
    row = lax.broadcasted_iota(jnp.int32, u.shape, 0)
    u1 = pltpu.roll(u, 1, 0)
    u1 = jnp.where(row == 0, prev[7:8, :], u1)
    u2 = pltpu.roll(u, 2, 0)
    u2 = jnp.where(row == 0, prev[6:7, :], jnp.where(row == 1, prev[7:8, :], u2))
    cw = cw_ref[...]
    conv = cw[0:1, :] * u2 + cw[1:2, :] * u1 + cw[2:3, :] * u
    y_a = (ab_ref[0].astype(F32) * conv).astype(BF16)
    o_ref[0] = (h_ref[0]
                + jnp.dot(y_a, wa_ref[...], preferred_element_type=F32)
                + jnp.dot(yb_ref[0], wb_ref[...], preferred_element_type=F32))


def conv_out(proj, y_b, conv_w, w_out_a, w_out_b, h, *, a_width, tm):
    bsz, s_len, d = h.shape
    halo = tm // 8
    return pl.pallas_call(
        functools.partial(_conv_out_kernel, tm=tm),
        grid=(bsz, s_len // tm),
        in_specs=[
            pl.BlockSpec((1, tm, a_width), lambda b, i: (b, i, 0)),
            pl.BlockSpec((1, tm, a_width), lambda b, i: (b, i, 1)),
            pl.BlockSpec((1, tm, a_width), lambda b, i: (b, i, 2)),
            pl.BlockSpec((1, 8, a_width), lambda b, i: (b, jnp.maximum(i * halo - 1, 0), 1)),
            pl.BlockSpec((1, 8, a_width), lambda b, i: (b, jnp.maximum(i * halo - 1, 0), 2)),
            pl.BlockSpec((1, tm, y_b.shape[-1]), lambda b, i: (b, i, 0)),
            pl.BlockSpec(conv_w.shape, lambda b, i: (0, 0)),
            pl.BlockSpec(w_out_a.shape, lambda b, i: (0, 0)),
            pl.BlockSpec(w_out_b.shape, lambda b, i: (0, 0)),
            pl.BlockSpec((1, tm, d), lambda b, i: (b, i, 0)),
        ],
        out_specs=pl.BlockSpec((1, tm, d), lambda b, i: (b, i, 0)),
        out_shape=jax.ShapeDtypeStruct((bsz, s_len, d), F32),
        compiler_params=_cparams(("parallel", "arbitrary")),
        name="conv_out",
    )(proj, proj, proj, proj, proj, y_b, conv_w, w_out_a, w_out_b, h)


def _level_ids(c):
    t = np.arange(c)[:, None]
    s = np.arange(c)[None, :]
    x = np.bitwise_xor(t, s)
    n_levels = int(math.log2(c))
    msb = np.floor(np.log2(np.maximum(x, 1))).astype(np.int64)
    lvl = (n_levels - 1) - msb
    lvl = np.where(x == 0, n_levels, lvl)
    lvl = np.where(s > t, -1, lvl)
    return lvl.astype(np.int32)


def _hgrn2_kernel(q_ref, f_ref, i_ref, g_ref, lb_ref, on_ref, lvl_ref, tri_ref, o_ref,
                  state_ref, b_ref, *, c, n_heads, dk):
    n_levels = int(math.log2(c))

    @pl.when(pl.program_id(1) == 0)
    def _():
        state_ref[...] = jnp.zeros_like(state_ref)

    lvl = lvl_ref[...]
    tri = tri_ref[...]
    row = lax.broadcasted_iota(jnp.int32, (c, dk), 0)
    nt = (((1,), (1,)), ((), ()))
    tn = (((0,), (0,)), ((), ()))

    def head(hd, carry):
        col = pl.multiple_of(hd * dk, dk)
        q_raw = q_ref[0, :, pl.ds(col, dk)].astype(F32)
        f_raw = f_ref[0, :, pl.ds(col, dk)].astype(F32)
        v = i_ref[0, :, pl.ds(col, dk)]
        g_raw = g_ref[0, :, pl.ds(col, dk)].astype(F32)
        lb = lb_ref[:, pl.ds(col, dk)]

        e = jnp.exp(-jnp.abs(f_raw))
        s = 1.0 / (1.0 + e)
        es = e * s
        pos = f_raw >= 0
        sig_f = jnp.where(pos, s, es)
        sig_nf = jnp.where(pos, es, s)
        fgate = lb + (1.0 - lb) * sig_f
        k = (1.0 - lb) * sig_nf
        x = jnp.log2(fgate)
        q = q_raw * _sigmoid(q_raw)

        x_hi = x.astype(BF16)
        r1 = x - x_hi.astype(F32)
        x_mid = r1.astype(BF16)
        x_lo = (r1 - x_mid.astype(F32)).astype(BF16)
        x3 = jnp.concatenate([x_hi, x_mid, x_lo], axis=1)
        b3 = jnp.dot(tri, x3, preferred_element_type=F32)
        b = b3[:, :dk] + b3[:, dk:2 * dk] + b3[:, 2 * dk:]
        b_ref[...] = b

        q_bf = q.astype(BF16)
        k_bf = k.astype(BF16)
        diag = lax.dot_general(q_bf, k_bf, nt, preferred_element_type=F32).astype(BF16)
        scores = jnp.where(lvl == n_levels, diag, jnp.zeros_like(diag))
        for level in range(n_levels):
            blk = c >> level
            half = blk // 2
            if blk >= 8:
                refs = [jnp.broadcast_to(b_ref[pl.ds(m * blk + half - 1, 1), :], (blk, dk))
                        for m in range(c // blk)]
                ref = refs[0] if len(refs) == 1 else jnp.concatenate(refs, axis=0)
                gl = jnp.exp2(-jnp.abs(b - ref))
            elif blk == 4:
                p = row & 3
                f_next = pltpu.roll(fgate, c - 1, 0)
                f_prev = pltpu.roll(fgate, 1, 0)
                gl = jnp.where(p == 0, f_next,
                               jnp.where(p == 1, 1.0,
                                         jnp.where(p == 2, fgate, fgate * f_prev)))
            else:
                gl = jnp.where((row & 1) == 1, fgate, 1.0)
            gl = gl.astype(BF16)
            sl = lax.dot_general(q_bf * gl, k_bf * gl, nt, preferred_element_type=F32)
            scores = jnp.where(lvl == level, sl.astype(BF16), scores)

        o_intra = jnp.dot(scores, v, preferred_element_type=F32)

        state = state_ref[hd]
        decay_in = jnp.exp2(b)
        o_inter = jnp.dot((q * decay_in).astype(BF16), state.astype(BF16),
                          preferred_element_type=F32)
        b_last = b_ref[pl.ds(c - 1, 1), :]
        k_out = (k * jnp.exp2(b_last - b)).astype(BF16)
        kv = lax.dot_general(k_out, v, tn, preferred_element_type=F32)
        eye = (lax.broadcasted_iota(jnp.int32, (dk, dk), 0)
               == lax.broadcasted_iota(jnp.int32, (dk, dk), 1))
        scale_col = jnp.sum(jnp.where(eye, jnp.exp2(b_last), 0.0), axis=1, keepdims=True)
        state_ref[hd] = state * scale_col + kv

        o = o_inter + o_intra
        o = o * lax.rsqrt(jnp.mean(o * o, axis=-1, keepdims=True) + RMS_EPS) * on_ref[...]
        o_ref[0, :, pl.ds(col, dk)] = (o * (g_raw * _sigmoid(g_raw))).astype(o_ref.dtype)
        return carry

    lax.fori_loop(0, n_heads, head, 0, unroll=2)


def hgrn2(proj, lower_bound, out_norm, *, n_heads, dk, c):
    bsz, s_len, _ = proj.shape
    w = n_heads * dk
    assert dk == LANES and s_len % c == 0 and (c & (c - 1)) == 0 and c >= 8
    lvl = jnp.asarray(_level_ids(c), BF16)
    tri = jnp.asarray(np.tril(np.ones((c, c), np.float32)), BF16)
    return pl.pallas_call(
        functools.partial(_hgrn2_kernel, c=c, n_heads=n_heads, dk=dk),
        grid=(bsz, s_len // c),
        in_specs=[
            pl.BlockSpec((1, c, w), lambda b, i: (b, i, 0)),
            pl.BlockSpec((1, c, w), lambda b, i: (b, i, 1)),
            pl.BlockSpec((1, c, w), lambda b, i: (b, i, 2)),
            pl.BlockSpec((1, c, w), lambda b, i: (b, i, 3)),
            pl.BlockSpec((1, w), lambda b, i: (0, 0)),
            pl.BlockSpec((1, dk), lambda b, i: (0, 0)),
            pl.BlockSpec((c, c), lambda b, i: (0, 0)),
            pl.BlockSpec((c, c), lambda b, i: (0, 0)),
        ],
        out_specs=pl.BlockSpec((1, c, w), lambda b, i: (b, i, 0)),
        out_shape=jax.ShapeDtypeStruct((bsz, s_len, w), BF16),
        scratch_shapes=[pltpu.VMEM((n_heads, dk, dk), F32), pltpu.VMEM((c, dk), F32)],
        compiler_params=_cparams(("parallel", "arbitrary")),
        name="hgrn2",
    )(proj, proj, proj, proj, lower_bound, out_norm, lvl, tri)


def _tiles(n_tokens, d_ff):
    tm = min(1024, n_tokens)
    tf = 256 if d_ff % 256 == 0 else d_ff
    return tm, tf


def trunk(x, ffn_pre_norm, ffn_pre_w_gate, ffn_pre_w_up, ffn_pre_w_down, mix_norm,
          ffn_post_norm, ffn_post_w_gate, ffn_post_w_up, ffn_post_w_down,
          ab_w_in, ab_conv_w, ab_w_out, c_w_in, c_lower_bounds, c_out_norm, c_w_out,
          final_norm, *, sb_heads, sb_head_dim, c_heads, c_head_dim, sb_tq, c_chunk):
    bsz, s_len, d = x.shape
    depth = ffn_pre_norm.shape[0]
    n = bsz * s_len
    d_ff = ffn_pre_w_gate.shape[-1]
    tm, tf = _tiles(n, d_ff)
    tm_seq = min(tm, s_len)
    bf = lambda t: t.astype(BF16)
    row = lambda t: t.reshape(1, -1).astype(F32)

    lb_soft = jax.nn.softmax(c_lower_bounds.astype(F32), axis=0)
    lb_cum = jnp.cumsum(lb_soft, axis=0)
    lower_bounds = lb_cum - lb_cum[0:1]

    a_width = ab_w_in.shape[-1] // 6
    h = x.reshape(n, d)
    for layer in range(depth):
        h = ffn(h, row(ffn_pre_norm[layer]), bf(ffn_pre_w_gate[layer]), bf(ffn_pre_w_up[layer]),
                bf(ffn_pre_w_down[layer]), row(final_norm), final_norm=False, tm=tm, tf=tf)
        if layer % 2 == 0:
            e = layer // 2
            proj = norm_proj(h, row(mix_norm[layer]), bf(ab_w_in[e]), tm=tm, tn=512)
            proj = proj.reshape(bsz, s_len, -1)
            y_b = sb_attention(proj, n_heads=sb_heads, dh=sb_head_dim, q_col=3 * a_width,
                               k_col=3 * a_width + sb_heads * sb_head_dim,
                               v_col=3 * a_width + 2 * sb_heads * sb_head_dim, tq=sb_tq)
            w_out = bf(ab_w_out[e])
            h = conv_out(proj, y_b, ab_conv_w[e].astype(F32), w_out[:a_width], w_out[a_width:],
                         h.reshape(bsz, s_len, d), a_width=a_width, tm=tm_seq).reshape(n, d)
        else:
            o = layer // 2
            proj = norm_proj(h, row(mix_norm[layer]), bf(c_w_in[o]), tm=tm, tn=512)
            proj = proj.reshape(bsz, s_len, -1)
            y = hgrn2(proj, row(lower_bounds[layer]), row(c_out_norm[o]),
                      n_heads=c_heads, dk=c_head_dim, c=c_chunk)
            h = out_proj(y.reshape(n, -1), bf(c_w_out[o]), h, tm=tm)
        h = ffn(h, row(ffn_post_norm[layer]), bf(ffn_post_w_gate[layer]), bf(ffn_post_w_up[layer]),
                bf(ffn_post_w_down[layer]), row(final_norm),
                final_norm=(layer == depth - 1), tm=tm, tf=tf)
    return h.reshape(bsz, s_len, d)


def kernel(x, ffn_pre_norm, ffn_pre_w_gate, ffn_pre_w_up, ffn_pre_w_down, mix_norm, ffn_post_norm,
           ffn_post_w_gate, ffn_post_w_up, ffn_post_w_down, ab_w_in, ab_conv_w, ab_w_out, c_w_in,
           c_lower_bounds, c_out_norm, c_w_out, final_norm):
    return trunk(x, ffn_pre_norm, ffn_pre_w_gate, ffn_pre_w_up, ffn_pre_w_down, mix_norm,
                 ffn_post_norm, ffn_post_w_gate, ffn_post_w_up, ffn_post_w_down,
                 ab_w_in, ab_conv_w, ab_w_out, c_w_in, c_lower_bounds, c_out_norm, c_w_out,
                 final_norm, sb_heads=8, sb_head_dim=64, c_heads=8, c_head_dim=128,
                 sb_tq=256, c_chunk=256)
```

```python
import functools
import math

import numpy as np
import jax
import jax.numpy as jnp
from jax import lax
from jax.experimental import pallas as pl
from jax.experimental.pallas import tpu as pltpu

RMS_EPS = 1e-6
MACARON_WEIGHT = 0.5
LANES = 128
VMEM_LIMIT_BYTES = 56 * 1024 * 1024
SB_EXIT_LOGDECAY = 104.0
LOG2E = 1.4426950408889634

F32 = jnp.float32
BF16 = jnp.bfloat16


def _cparams(sem):
    return pltpu.CompilerParams(dimension_semantics=sem, vmem_limit_bytes=VMEM_LIMIT_BYTES)


def _rms_norm_f32(x, gain):
    return x * lax.rsqrt(jnp.mean(x * x, axis=-1, keepdims=True) + RMS_EPS) * gain


def _sigmoid(x):
    return 1.0 / (1.0 + jnp.exp(-x))


def _silu(x):
    hx = 0.5 * x
    return hx + hx * jnp.tanh(hx)


def _ffn_kernel(h_ref, gain_ref, wg_ref, wu_ref, wd_ref, fgain_ref, o_ref, xn_ref, acc_ref,
                *, final_norm, tf):
    xn_ref[...] = _rms_norm_f32(h_ref[...], gain_ref[...]).astype(BF16)
    acc_ref[...] = jnp.zeros_like(acc_ref)

    def chunk(j, carry):
        col = pl.multiple_of(j * tf, tf)
        xn = xn_ref[...]
        g = jnp.dot(xn, wg_ref[:, pl.ds(col, tf)], preferred_element_type=F32)
        u = jnp.dot(xn, wu_ref[:, pl.ds(col, tf)], preferred_element_type=F32)
        a = (g * _sigmoid(g) * u).astype(BF16)
        acc_ref[...] += jnp.dot(a, wd_ref[pl.ds(col, tf), :], preferred_element_type=F32)
        return carry

    lax.fori_loop(0, wg_ref.shape[1] // tf, chunk, 0, unroll=True)
    out = h_ref[...] + MACARON_WEIGHT * acc_ref[...]
    if final_norm:
        out = _rms_norm_f32(out, fgain_ref[...])
    o_ref[...] = out


def ffn(h, gain, wg, wu, wd, fgain, *, final_norm, tm, tf):
    n, d = h.shape
    f = wg.shape[1]
    resident = lambda shape: pl.BlockSpec(shape, lambda i: (0, 0), pipeline_mode=pl.Buffered(1))
    return pl.pallas_call(
        functools.partial(_ffn_kernel, final_norm=final_norm, tf=tf),
        grid=(n // tm,),
        in_specs=[
            pl.BlockSpec((tm, d), lambda i: (i, 0)),
            resident((1, d)),
            resident((d, f)),
            resident((d, f)),
            resident((f, d)),
            resident((1, d)),
        ],
        out_specs=pl.BlockSpec((tm, d), lambda i: (i, 0)),
        out_shape=jax.ShapeDtypeStruct((n, d), F32),
        scratch_shapes=[pltpu.VMEM((tm, d), BF16), pltpu.VMEM((tm, d), F32)],
        compiler_params=_cparams(("parallel",)),
        name="ffn",
    )(h, gain, wg, wu, wd, fgain)


def _norm_proj_kernel(h_ref, gain_ref, w_ref, o_ref, xn_ref, *, tn):
    xn_ref[...] = _rms_norm_f32(h_ref[...], gain_ref[...]).astype(BF16)

    def chunk(j, carry):
        col = pl.multiple_of(j * tn, tn)
        o_ref[:, pl.ds(col, tn)] = jnp.dot(xn_ref[...], w_ref[:, pl.ds(col, tn)],
                                           preferred_element_type=F32).astype(o_ref.dtype)
        return carry

    lax.fori_loop(0, w_ref.shape[1] // tn, chunk, 0, unroll=True)


def norm_proj(h, gain, w, *, tm, tn):
    n, d = h.shape
    c = w.shape[1]
    resident = lambda shape: pl.BlockSpec(shape, lambda i: (0, 0), pipeline_mode=pl.Buffered(1))
    return pl.pallas_call(
        functools.partial(_norm_proj_kernel, tn=tn),
        grid=(n // tm,),
        in_specs=[pl.BlockSpec((tm, d), lambda i: (i, 0)), resident((1, d)), resident((d, c))],
        out_specs=pl.BlockSpec((tm, c), lambda i: (i, 0)),
        out_shape=jax.ShapeDtypeStruct((n, c), BF16),
        scratch_shapes=[pltpu.VMEM((tm, d), BF16)],
        compiler_params=_cparams(("parallel",)),
        name="norm_proj",
    )(h, gain, w)


def _out_proj_kernel(y_ref, w_ref, h_ref, o_ref):
    o_ref[...] = h_ref[...] + jnp.dot(y_ref[...], w_ref[...], preferred_element_type=F32)


def out_proj(y, w, h, *, tm):
    n, d = h.shape
    c = y.shape[1]
    return pl.pallas_call(
        _out_proj_kernel,
        grid=(n // tm,),
        in_specs=[
            pl.BlockSpec((tm, c), lambda i: (i, 0)),
            pl.BlockSpec((c, d), lambda i: (0, 0)),
            pl.BlockSpec((tm, d), lambda i: (i, 0)),
        ],
        out_specs=pl.BlockSpec((tm, d), lambda i: (i, 0)),
        out_shape=jax.ShapeDtypeStruct((n, d), F32),
        compiler_params=_cparams(("parallel",)),
        name="out_proj",
    )(y, w, h)


def _sb_attn_kernel(q_ref, k_ref, v_ref, tri_ref, o_ref, acc_ref, dec_ref, *, tq, dh, scale, n_q):
    lane = lax.broadcasted_iota(jnp.int32, (tq, LANES), 1)
    head_b = lane >= dh
    tri = tri_ref[...]
    row = lax.broadcasted_iota(jnp.int32, (tq, tq), 0)
    col = lax.broadcasted_iota(jnp.int32, (tq, tq), 1)
    causal = col < row
    nt = (((1,), (1,)), ((), ()))

    def key_block(j):
        start = pl.multiple_of(j * tq, tq)
        return k_ref[0, pl.ds(start, tq), :], v_ref[0, pl.ds(start, tq), :]

    def head_block(qh, kb, vb, dec, diagonal):
        z = lax.dot_general(qh, kb, nt, preferred_element_type=F32)
        sp = jnp.maximum(z, 0.0) + jnp.log2(1.0 + jnp.exp2(-jnp.abs(z)))
        if diagonal:
            sp = jnp.where(causal, sp, 0.0)
        later = jnp.dot(sp.astype(BF16), tri, preferred_element_type=F32)
        arg = z - later
        total = later[:, 0:1]
        if dec is not None:
            arg = arg - dec
            total = total + dec
        w = jnp.exp2(arg)
        if diagonal:
            w = jnp.where(causal, w, 0.0)
        return jnp.dot(w.astype(BF16), vb, preferred_element_type=F32), total

    def q_heads_of(i):
        qs = pl.multiple_of(i * tq, tq)
        q = (q_ref[0, pl.ds(qs, tq), :].astype(F32) * scale).astype(BF16)
        zero = jnp.zeros_like(q)
        return jnp.where(head_b, zero, q), jnp.where(head_b, q, zero)

    def head_blocks(i, n_static):
        q_heads = q_heads_of(i)
        blocks = [key_block(i - d) for d in range(n_static)]
        outs = []
        for hh in range(2):
            out, dec = head_block(q_heads[hh], *blocks[0], None, True)
            for kb, vb in blocks[1:]:
                more, dec = head_block(q_heads[hh], kb, vb, dec, False)
                out = out + more
            outs.append(out)
            dec_ref[hh] = dec
        acc_ref[...] = jnp.where(head_b, outs[1], outs[0])

    def finish(i, n_static):
        def undecayed():
            return jnp.min(dec_ref[...]) < SB_EXIT_LOGDECAY * LOG2E

        def cond(state):
            j, go = state
            return jnp.logical_and(j >= 0, go)

        def body(state):
            j, _ = state
            q_heads = q_heads_of(i)
            kb, vb = key_block(j)
            more = []
            for hh in range(2):
                out, dec = head_block(q_heads[hh], kb, vb, dec_ref[hh], False)
                dec_ref[hh] = dec
                more.append(out)
            acc_ref[...] += jnp.where(head_b, more[1], more[0])
            return j - 1, undecayed()

        lax.while_loop(cond, body, (i - n_static, undecayed()))
        qs = pl.multiple_of(i * tq, tq)
        o_ref[0, pl.ds(qs, tq), :] = acc_ref[...].astype(o_ref.dtype)

    first = jnp.int32(0)
    head_blocks(first, 1)
    finish(first, 1)

    def rest(i, carry):
        head_blocks(i, 2)
        finish(i, 2)
        return carry

    lax.fori_loop(1, n_q, rest, 0)


def sb_attention(proj, *, n_heads, dh, q_col, k_col, v_col, tq):
    bsz, s_len, _ = proj.shape
    assert 2 * dh == LANES and n_heads % 2 == 0
    assert q_col % LANES == 0 and k_col % LANES == 0 and v_col % LANES == 0
    pairs = n_heads // 2
    qb, kb, vb = q_col // LANES, k_col // LANES, v_col // LANES
    tri = jnp.asarray(np.tril(np.ones((tq, tq), np.float32)), BF16)
    seq = lambda c: pl.BlockSpec((1, s_len, LANES), lambda b, p: (b, 0, c + p))
    return pl.pallas_call(
        functools.partial(_sb_attn_kernel, tq=tq, dh=dh, scale=LOG2E / math.sqrt(dh),
                          n_q=s_len // tq),
        grid=(bsz, pairs),
        in_specs=[seq(qb), seq(kb), seq(vb), pl.BlockSpec((tq, tq), lambda b, p: (0, 0))],
        out_specs=seq(0),
        out_shape=jax.ShapeDtypeStruct((bsz, s_len, pairs * LANES), BF16),
        scratch_shapes=[pltpu.VMEM((tq, LANES), F32), pltpu.VMEM((2, tq, 1), F32)],
        compiler_params=_cparams(("parallel", "parallel")),
        name="sb_attn",
    )(proj, proj, proj, tri)


def _conv_out_kernel(ab_ref, ac_ref, ax_ref, pc_ref, px_ref, yb_ref, cw_ref, wa_ref, wb_ref, h_ref,
                     o_ref, *, tm):
    i = pl.program_id(1)
    u = ac_ref[0].astype(F32) * ax_ref[0].astype(F32)
    prev = pc_ref[0].astype(F32) * px_ref[0].astype(F32)
    prev = jnp.where(i == 0, 0.0, prev)
    row = lax.broadcasted_iota(jnp.int32, u.shape, 0)
    u1 = pltpu.roll(u, 1, 0)
    u1 = jnp.where(row == 0, prev[7:8, :], u1)
    u2 = pltpu.roll(u, 2, 0)
    u2 = jnp.where(row == 0, prev[6:7, :], jnp.where(row == 1, prev[7:8, :], u2))
    cw = cw_ref[...]
    conv = cw[0:1, :] * u2 + cw[1:2, :] * u1 + cw[2:3, :] * u
    y_a = (ab_ref[0].astype(F32) * conv).astype(BF16)
    o_ref[0] = (h_ref[0]
                + jnp.dot(y_a, wa_ref[...], preferred_element_type=F32)
                + jnp.dot(yb_ref[0], wb_ref[...], preferred_element_type=F32))


def conv_out(proj, y_b, conv_w, w_out_a, w_out_b, h, *, a_width, tm):
    bsz, s_len, d = h.shape
    halo = tm // 8
    return pl.pallas_call(
        functools.partial(_conv_out_kernel, tm=tm),
        grid=(bsz, s_len // tm),
        in_specs=[
            pl.BlockSpec((1, tm, a_width), lambda b, i: (b, i, 0)),
            pl.BlockSpec((1, tm, a_width), lambda b, i: (b, i, 1)),
            pl.BlockSpec((1, tm, a_width), lambda b, i: (b, i, 2)),
            pl.BlockSpec((1, 8, a_width), lambda b, i: (b, jnp.maximum(i * halo - 1, 0), 1)),
            pl.BlockSpec((1, 8, a_width), lambda b, i: (b, jnp.maximum(i * halo - 1, 0), 2)),
            pl.BlockSpec((1, tm, y_b.shape[-1]), lambda b, i: (b, i, 0)),
            pl.BlockSpec(conv_w.shape, lambda b, i: (0, 0)),
            pl.BlockSpec(w_out_a.shape, lambda b, i: (0, 0)),
            pl.BlockSpec(w_out_b.shape, lambda b, i: (0, 0)),
            pl.BlockSpec((1, tm, d), lambda b, i: (b, i, 0)),
        ],
        out_specs=pl.BlockSpec((1, tm, d), lambda b, i: (b, i, 0)),
        out_shape=jax.ShapeDtypeStruct((bsz, s_len, d), F32),
        compiler_params=_cparams(("parallel", "arbitrary")),
        name="conv_out",
    )(proj, proj, proj, proj, proj, y_b, conv_w, w_out_a, w_out_b, h)


def _level_ids(c):
    t = np.arange(c)[:, None]
    s = np.arange(c)[None, :]
    x = np.bitwise_xor(t, s)
    n_levels = int(math.log2(c))
    msb = np.floor(np.log2(np.maximum(x, 1))).astype(np.int64)
    lvl = (n_levels - 1) - msb
    lvl = np.where(x == 0, n_levels, lvl)
    lvl = np.where(s > t, -1, lvl)
    return lvl.astype(np.int32)


def _hgrn2_kernel(q_ref, f_ref, i_ref, g_ref, lb_ref, on_ref, lvl_ref, tri_ref, o_ref,
                  state_ref, b_ref, *, c, n_heads, dk):
    n_levels = int(math.log2(c))

    @pl.when(pl.program_id(1) == 0)
    def _():
        state_ref[...] = jnp.zeros_like(state_ref)

    lvl = lvl_ref[...]
    tri = tri_ref[...]
    row = lax.broadcasted_iota(jnp.int32, (c, dk), 0)
    nt = (((1,), (1,)), ((), ()))
    tn = (((0,), (0,)), ((), ()))

    def head(hd, carry):
        col = pl.multiple_of(hd * dk, dk)
        q_raw = q_ref[0, :, pl.ds(col, dk)].astype(F32)
        f_raw = f_ref[0, :, pl.ds(col, dk)].astype(F32)
        v = i_ref[0, :, pl.ds(col, dk)]
        g_raw = g_ref[0, :, pl.ds(col, dk)].astype(F32)
        lb = lb_ref[:, pl.ds(col, dk)]

        c1 = 0.5 - 0.5 * lb
        c1t = c1 * jnp.tanh(0.5 * f_raw)
        fgate = (lb + c1) + c1t
        k = c1 - c1t
        x = jnp.log2(fgate)
        q = _silu(q_raw)

        x_hi = x.astype(BF16)
        x_lo = (x - x_hi.astype(F32)).astype(BF16)
        b2 = jnp.dot(tri, jnp.concatenate([x_hi, x_lo], axis=1), preferred_element_type=F32)
        b = b2[:, :dk] + b2[:, dk:]
        b_ref[...] = b

        q_bf = q.astype(BF16)
        k_bf = k.astype(BF16)
        diag = lax.dot_general(q_bf, k_bf, nt, preferred_element_type=F32).astype(BF16)
        scores = jnp.where(lvl == n_levels, diag, jnp.zeros_like(diag))
        for level in range(n_levels):
            blk = c >> level
            half = blk // 2
            if blk >= 8:
                refs = [jnp.broadcast_to(b_ref[pl.ds(m * blk + half - 1, 1), :], (blk, dk))
                        for m in range(c // blk)]
                ref = refs[0] if len(refs) == 1 else jnp.concatenate(refs, axis=0)
                gl = jnp.exp2(-jnp.abs(b - ref))
            elif blk == 4:
                p = row & 3
                f_next = pltpu.roll(fgate, c - 1, 0)
                f_prev = pltpu.roll(fgate, 1, 0)
                gl = jnp.where(p == 0, f_next,
                               jnp.where(p == 1, 1.0,
                                         jnp.where(p == 2, fgate, fgate * f_prev)))
            else:
                gl = jnp.where((row & 1) == 1, fgate, 1.0)
            gl = gl.astype(BF16)
            sl = lax.dot_general(q_bf * gl, k_bf * gl, nt, preferred_element_type=F32)
            scores = jnp.where(lvl == level, sl.astype(BF16), scores)

        o_intra = jnp.dot(scores, v, preferred_element_type=F32)

        state = state_ref[hd]
        decay_in = jnp.exp2(b)
        o_inter = jnp.dot((q * decay_in).astype(BF16), state.astype(BF16),
                          preferred_element_type=F32)
        b_last = b_ref[pl.ds(c - 1, 1), :]
        k_out = (k * jnp.exp2(b_last - b)).astype(BF16)
        kv = lax.dot_general(k_out, v, tn, preferred_element_type=F32)
        eye = (lax.broadcasted_iota(jnp.int32, (dk, dk), 0)
               == lax.broadcasted_iota(jnp.int32, (dk, dk), 1))
        scale_col = jnp.sum(jnp.where(eye, jnp.exp2(b_last), 0.0), axis=1, keepdims=True)
        state_ref[hd] = state * scale_col + kv

        o = o_inter + o_intra
        o = o * lax.rsqrt(jnp.mean(o * o, axis=-1, keepdims=True) + RMS_EPS) * on_ref[...]
        o_ref[0, :, pl.ds(col, dk)] = (o * _silu(g_raw)).astype(o_ref.dtype)
        return carry

    lax.fori_loop(0, n_heads, head, 0, unroll=4)


def hgrn2(proj, lower_bound, out_norm, *, n_heads, dk, c):
    bsz, s_len, _ = proj.shape
    w = n_heads * dk
    assert dk == LANES and s_len % c == 0 and (c & (c - 1)) == 0 and c >= 8
    lvl = jnp.asarray(_level_ids(c), BF16)
    tri = jnp.asarray(np.tril(np.ones((c, c), np.float32)), BF16)
    return pl.pallas_call(
        functools.partial(_hgrn2_kernel, c=c, n_heads=n_heads, dk=dk),
        grid=(bsz, s_len // c),
        in_specs=[
            pl.BlockSpec((1, c, w), lambda b, i: (b, i, 0)),
            pl.BlockSpec((1, c, w), lambda b, i: (b, i, 1)),
            pl.BlockSpec((1, c, w), lambda b, i: (b, i, 2)),
            pl.BlockSpec((1, c, w), lambda b, i: (b, i, 3)),
            pl.BlockSpec((1, w), lambda b, i: (0, 0)),
            pl.BlockSpec((1, dk), lambda b, i: (0, 0)),
            pl.BlockSpec((c, c), lambda b, i: (0, 0)),
            pl.BlockSpec((c, c), lambda b, i: (0, 0)),
        ],
        out_specs=pl.BlockSpec((1, c, w), lambda b, i: (b, i, 0)),
        out_shape=jax.ShapeDtypeStruct((bsz, s_len, w), BF16),
        scratch_shapes=[pltpu.VMEM((n_heads, dk, dk), F32), pltpu.VMEM((c, dk), F32)],
        compiler_params=_cparams(("parallel", "arbitrary")),
        name="hgrn2",
    )(proj, proj, proj, proj, lower_bound, out_norm, lvl, tri)


def _tiles(n_tokens, d_ff):
    tm = min(1024, n_tokens)
    tf = 256 if d_ff % 256 == 0 else d_ff
    return tm, tf


def trunk(x, ffn_pre_norm, ffn_pre_w_gate, ffn_pre_w_up, ffn_pre_w_down, mix_norm,
          ffn_post_norm, ffn_post_w_gate, ffn_post_w_up, ffn_post_w_down,
          ab_w_in, ab_conv_w, ab_w_out, c_w_in, c_lower_bounds, c_out_norm, c_w_out,
          final_norm, *, sb_heads, sb_head_dim, c_heads, c_head_dim, sb_tq, c_chunk):
    bsz, s_len, d = x.shape
    depth = ffn_pre_norm.shape[0]
    n = bsz * s_len
    d_ff = ffn_pre_w_gate.shape[-1]
    tm, tf = _tiles(n, d_ff)
    tm_seq = min(tm, s_len)
    bf = lambda t: t.astype(BF16)
    row = lambda t: t.reshape(1, -1).astype(F32)

    lb_soft = jax.nn.softmax(c_lower_bounds.astype(F32), axis=0)
    lb_cum = jnp.cumsum(lb_soft, axis=0)
    lower_bounds = lb_cum - lb_cum[0:1]

    a_width = ab_w_in.shape[-1] // 6
    h = x.reshape(n, d)
    for layer in range(depth):
        h = ffn(h, row(ffn_pre_norm[layer]), bf(ffn_pre_w_gate[layer]), bf(ffn_pre_w_up[layer]),
                bf(ffn_pre_w_down[layer]), row(final_norm), final_norm=False, tm=tm, tf=tf)
        if layer % 2 == 0:
            e = layer // 2
            proj = norm_proj(h, row(mix_norm[layer]), bf(ab_w_in[e]), tm=tm, tn=512)
            proj = proj.reshape(bsz, s_len, -1)
            y_b = sb_attention(proj, n_heads=sb_heads, dh=sb_head_dim, q_col=3 * a_width,
                               k_col=3 * a_width + sb_heads * sb_head_dim,
                               v_col=3 * a_width + 2 * sb_heads * sb_head_dim, tq=sb_tq)
            w_out = bf(ab_w_out[e])
            h = conv_out(proj, y_b, ab_conv_w[e].astype(F32), w_out[:a_width], w_out[a_width:],
                         h.reshape(bsz, s_len, d), a_width=a_width, tm=tm_seq).reshape(n, d)
        else:
            o = layer // 2
            proj = norm_proj(h, row(mix_norm[layer]), bf(c_w_in[o]), tm=tm, tn=512)
            proj = proj.reshape(bsz, s_len, -1)
            y = hgrn2(proj, row(lower_bounds[layer]), row(c_out_norm[o]),
                      n_heads=c_heads, dk=c_head_dim, c=c_chunk)
            h = out_proj(y.reshape(n, -1), bf(c_w_out[o]), h, tm=tm)
        h = ffn(h, row(ffn_post_norm[layer]), bf(ffn_post_w_gate[layer]), bf(ffn_post_w_up[layer]),
                bf(ffn_post_w_down[layer]), row(final_norm),
                final_norm=(layer == depth - 1), tm=tm, tf=tf)
    return h.reshape(bsz, s_len, d)


def kernel(x, ffn_pre_norm, ffn_pre_w_gate, ffn_pre_w_up, ffn_pre_w_down, mix_norm, ffn_post_norm,
           ffn_post_w_gate, ffn_post_w_up, ffn_post_w_down, ab_w_in, ab_conv_w, ab_w_out, c_w_in,
           c_lower_bounds, c_out_norm, c_w_out, final_norm):
    return trunk(x, ffn_pre_norm, ffn_pre_w_gate, ffn_pre_w_up, ffn_pre_w_down, mix_norm,
                 ffn_post_norm, ffn_post_w_gate, ffn_post_w_up, ffn_post_w_down,
                 ab_w_in, ab_conv_w, ab_w_out, c_w_in, c_lower_bounds, c_out_norm, c_w_out,
                 final_norm, sb_heads=8, sb_head_dim=64, c_heads=8, c_head_dim=128,
                 sb_tq=256, c_chunk=256)
```

```python
import functools
import math

import numpy as np
import jax
import jax.numpy as jnp
from jax import lax
from jax.experimental import pallas as pl
from jax.experimental.pallas import tpu as pltpu

RMS_EPS = 1e-6
MACARON_WEIGHT = 0.5
LANES = 128
VMEM_LIMIT_BYTES = 56 * 1024 * 1024
SB_EXIT_LOGDECAY = 104.0
LOG2E = 1.4426950408889634

F32 = jnp.float32
BF16 = jnp.bfloat16


def _cparams(sem):
    return pltpu.CompilerParams(dimension_semantics=sem, vmem_limit_bytes=VMEM_LIMIT_BYTES)


def _rms_norm_f32(x, gain):
    return x * lax.rsqrt(jnp.mean(x * x, axis=-1, keepdims=True) + RMS_EPS) * gain


def _sigmoid(x):
    return 1.0 / (1.0 + jnp.exp(-x))


def _silu(x):
    hx = 0.5 * x
    return hx + hx * jnp.tanh(hx)


def _mix_ffn_kernel(y_ref, wm_ref, h_ref, gain_ref, wg_ref, wu_ref, wd_ref, fgain_ref, o_ref,
                    xn_ref, acc_ref, *, final_norm, tf):
    o_ref[...] = h_ref[...] + jnp.dot(y_ref[...], wm_ref[...], preferred_element_type=F32)
    _ffn_kernel(o_ref, gain_ref, wg_ref, wu_ref, wd_ref, fgain_ref, o_ref, xn_ref, acc_ref,
                final_norm=final_norm, tf=tf)


def _ffn_kernel(h_ref, gain_ref, wg_ref, wu_ref, wd_ref, fgain_ref, o_ref, xn_ref, acc_ref,
                *, final_norm, tf):
    xn_ref[...] = _rms_norm_f32(h_ref[...], gain_ref[...]).astype(BF16)
    acc_ref[...] = jnp.zeros_like(acc_ref)

    def chunk(j, carry):
        col = pl.multiple_of(j * tf, tf)
        xn = xn_ref[...]
        g = jnp.dot(xn, wg_ref[:, pl.ds(col, tf)], preferred_element_type=F32)
        u = jnp.dot(xn, wu_ref[:, pl.ds(col, tf)], preferred_element_type=F32)
        a = (g * _sigmoid(g) * u).astype(BF16)
        acc_ref[...] += jnp.dot(a, wd_ref[pl.ds(col, tf), :], preferred_element_type=F32)
        return carry

    lax.fori_loop(0, wg_ref.shape[1] // tf, chunk, 0, unroll=True)
    out = h_ref[...] + MACARON_WEIGHT * acc_ref[...]
    if final_norm:
        out = _rms_norm_f32(out, fgain_ref[...])
    o_ref[...] = out


def ffn(h, gain, wg, wu, wd, fgain, *, final_norm, tm, tf, mix=None):
    n, d = h.shape
    f = wg.shape[1]
    resident = lambda shape: pl.BlockSpec(shape, lambda i: (0, 0), pipeline_mode=pl.Buffered(1))
    tile = lambda width: pl.BlockSpec((tm, width), lambda i: (i, 0))
    in_specs = [tile(d), resident((1, d)), resident((d, f)), resident((d, f)), resident((f, d)),
                resident((1, d))]
    args = (h, gain, wg, wu, wd, fgain)
    body = _ffn_kernel
    if mix is not None:
        y, wm = mix
        in_specs = [tile(y.shape[1]), resident(wm.shape)] + in_specs
        args = (y, wm) + args
        body = _mix_ffn_kernel
    return pl.pallas_call(
        functools.partial(body, final_norm=final_norm, tf=tf),
        grid=(n // tm,),
        in_specs=in_specs,
        out_specs=tile(d),
        out_shape=jax.ShapeDtypeStruct((n, d), F32),
        scratch_shapes=[pltpu.VMEM((tm, d), BF16), pltpu.VMEM((tm, d), F32)],
        compiler_params=_cparams(("parallel",)),
        name="ffn" if mix is None else "mix_ffn",
    )(*args)


def _norm_proj_kernel(h_ref, gain_ref, w_ref, o_ref, xn_ref, *, tn):
    xn_ref[...] = _rms_norm_f32(h_ref[...], gain_ref[...]).astype(BF16)

    def chunk(j, carry):
        col = pl.multiple_of(j * tn, tn)
        o_ref[:, pl.ds(col, tn)] = jnp.dot(xn_ref[...], w_ref[:, pl.ds(col, tn)],
                                           preferred_element_type=F32).astype(o_ref.dtype)
        return carry

    lax.fori_loop(0, w_ref.shape[1] // tn, chunk, 0, unroll=True)


def norm_proj(h, gain, w, *, tm, tn):
    n, d = h.shape
    c = w.shape[1]
    resident = lambda shape: pl.BlockSpec(shape, lambda i: (0, 0), pipeline_mode=pl.Buffered(1))
    return pl.pallas_call(
        functools.partial(_norm_proj_kernel, tn=tn),
        grid=(n // tm,),
        in_specs=[pl.BlockSpec((tm, d), lambda i: (i, 0)), resident((1, d)), resident((d, c))],
        out_specs=pl.BlockSpec((tm, c), lambda i: (i, 0)),
        out_shape=jax.ShapeDtypeStruct((n, c), BF16),
        scratch_shapes=[pltpu.VMEM((tm, d), BF16)],
        compiler_params=_cparams(("parallel",)),
        name="norm_proj",
    )(h, gain, w)


def _sb_attn_kernel(q_ref, k_ref, v_ref, tri_ref, o_ref, acc_ref, dec_ref, *, tq, dh, scale, n_q):
    lane = lax.broadcasted_iota(jnp.int32, (tq, LANES), 1)
    head_b = lane >= dh
    tri = tri_ref[...]
    row = lax.broadcasted_iota(jnp.int32, (tq, tq), 0)
    col = lax.broadcasted_iota(jnp.int32, (tq, tq), 1)
    causal = col < row
    nt = (((1,), (1,)), ((), ()))

    def key_block(j):
        start = pl.multiple_of(j * tq, tq)
        return k_ref[0, pl.ds(start, tq), :], v_ref[0, pl.ds(start, tq), :]

    def head_block(qh, kb, vb, dec, diagonal):
        z = lax.dot_general(qh, kb, nt, preferred_element_type=F32)
        sp = jnp.maximum(z, 0.0) + jnp.log2(1.0 + jnp.exp2(-jnp.abs(z)))
        if diagonal:
            sp = jnp.where(causal, sp, 0.0)
        later = jnp.dot(sp.astype(BF16), tri, preferred_element_type=F32)
        arg = z - later
        total = later[:, 0:1]
        if dec is not None:
            arg = arg - dec
            total = total + dec
        w = jnp.exp2(arg)
        if diagonal:
            w = jnp.where(causal, w, 0.0)
        return jnp.dot(w.astype(BF16), vb, preferred_element_type=F32), total

    def q_heads_of(i):
        qs = pl.multiple_of(i * tq, tq)
        q = (q_ref[0, pl.ds(qs, tq), :].astype(F32) * scale).astype(BF16)
        zero = jnp.zeros_like(q)
        return jnp.where(head_b, zero, q), jnp.where(head_b, q, zero)

    def head_blocks(i, n_static):
        q_heads = q_heads_of(i)
        blocks = [key_block(i - d) for d in range(n_static)]
        outs = []
        for hh in range(2):
            out, dec = head_block(q_heads[hh], *blocks[0], None, True)
            for kb, vb in blocks[1:]:
                more, dec = head_block(q_heads[hh], kb, vb, dec, False)
                out = out + more
            outs.append(out)
            dec_ref[hh] = dec
        acc_ref[...] = jnp.where(head_b, outs[1], outs[0])

    def finish(i, n_static):
        def undecayed():
            return jnp.min(dec_ref[...]) < SB_EXIT_LOGDECAY * LOG2E

        def cond(state):
            j, go = state
            return jnp.logical_and(j >= 0, go)

        def body(state):
            j, _ = state
            q_heads = q_heads_of(i)
            kb, vb = key_block(j)
            more = []
            for hh in range(2):
                out, dec = head_block(q_heads[hh], kb, vb, dec_ref[hh], False)
                dec_ref[hh] = dec
                more.append(out)
            acc_ref[...] += jnp.where(head_b, more[1], more[0])
            return j - 1, undecayed()

        lax.while_loop(cond, body, (i - n_static, undecayed()))
        qs = pl.multiple_of(i * tq, tq)
        o_ref[0, pl.ds(qs, tq), :] = acc_ref[...].astype(o_ref.dtype)

    first = jnp.int32(0)
    head_blocks(first, 1)
    finish(first, 1)

    def rest(i, carry):
        head_blocks(i, 2)
        finish(i, 2)
        return carry

    lax.fori_loop(1, n_q, rest, 0)


def sb_attention(proj, *, n_heads, dh, q_col, k_col, v_col, tq):
    bsz, s_len, _ = proj.shape
    assert 2 * dh == LANES and n_heads % 2 == 0
    assert q_col % LANES == 0 and k_col % LANES == 0 and v_col % LANES == 0
    pairs = n_heads // 2
    qb, kb, vb = q_col // LANES, k_col // LANES, v_col // LANES
    tri = jnp.asarray(np.tril(np.ones((tq, tq), np.float32)), BF16)
    seq = lambda c: pl.BlockSpec((1, s_len, LANES), lambda b, p: (b, 0, c + p))
    return pl.pallas_call(
        functools.partial(_sb_attn_kernel, tq=tq, dh=dh, scale=LOG2E / math.sqrt(dh),
                          n_q=s_len // tq),
        grid=(bsz, pairs),
        in_specs=[seq(qb), seq(kb), seq(vb), pl.BlockSpec((tq, tq), lambda b, p: (0, 0))],
        out_specs=seq(0),
        out_shape=jax.ShapeDtypeStruct((bsz, s_len, pairs * LANES), BF16),
        scratch_shapes=[pltpu.VMEM((tq, LANES), F32), pltpu.VMEM((2, tq, 1), F32)],
        compiler_params=_cparams(("parallel", "parallel")),
        name="sb_attn",
    )(proj, proj, proj, tri)


def _conv_out_kernel(ab_ref, ac_ref, ax_ref, pc_ref, px_ref, yb_ref, cw_ref, wa_ref, wb_ref, h_ref,
                     o_ref, *, tm):
    i = pl.program_id(1)
    u = ac_ref[0].astype(F32) * ax_ref[0].astype(F32)
    prev = pc_ref[0].astype(F32) * px_ref[0].astype(F32)
    prev = jnp.where(i == 0, 0.0, prev)
    row = lax.broadcasted_iota(jnp.int32, u.shape, 0)
    u1 = pltpu.roll(u, 1, 0)
    u1 = jnp.where(row == 0, prev[7:8, :], u1)
    u2 = pltpu.roll(u, 2, 0)
    u2 = jnp.where(row == 0, prev[6:7, :], jnp.where(row == 1, prev[7:8, :], u2))
    cw = cw_ref[...]
    conv = cw[0:1, :] * u2 + cw[1:2, :] * u1 + cw[2:3, :] * u
    y_a = (ab_ref[0].astype(F32) * conv).astype(BF16)
    o_ref[0] = (h_ref[0]
                + jnp.dot(y_a, wa_ref[...], preferred_element_type=F32)
                + jnp.dot(yb_ref[0], wb_ref[...], preferred_element_type=F32))


def conv_out(proj, y_b, conv_w, w_out_a, w_out_b, h, *, a_width, tm):
    bsz, s_len, d = h.shape
    halo = tm // 8
    return pl.pallas_call(
        functools.partial(_conv_out_kernel, tm=tm),
        grid=(bsz, s_len // tm),
        in_specs=[
            pl.BlockSpec((1, tm, a_width), lambda b, i: (b, i, 0)),
            pl.BlockSpec((1, tm, a_width), lambda b, i: (b, i, 1)),
            pl.BlockSpec((1, tm, a_width), lambda b, i: (b, i, 2)),
            pl.BlockSpec((1, 8, a_width), lambda b, i: (b, jnp.maximum(i * halo - 1, 0), 1)),
            pl.BlockSpec((1, 8, a_width), lambda b, i: (b, jnp.maximum(i * halo - 1, 0), 2)),
            pl.BlockSpec((1, tm, y_b.shape[-1]), lambda b, i: (b, i, 0)),
            pl.BlockSpec(conv_w.shape, lambda b, i: (0, 0)),
            pl.BlockSpec(w_out_a.shape, lambda b, i: (0, 0)),
            pl.BlockSpec(w_out_b.shape, lambda b, i: (0, 0)),
            pl.BlockSpec((1, tm, d), lambda b, i: (b, i, 0)),
        ],
        out_specs=pl.BlockSpec((1, tm, d), lambda b, i: (b, i, 0)),
        out_shape=jax.ShapeDtypeStruct((bsz, s_len, d), F32),
        compiler_params=_cparams(("parallel", "arbitrary")),
        name="conv_out",
    )(proj, proj, proj, proj, proj, y_b, conv_w, w_out_a, w_out_b, h)


def _level_ids(c):
    t = np.arange(c)[:, None]
    s = np.arange(c)[None, :]
    x = np.bitwise_xor(t, s)
    n_levels = int(math.log2(c))
    msb = np.floor(np.log2(np.maximum(x, 1))).astype(np.int64)
    lvl = (n_levels - 1) - msb
    lvl = np.where(x == 0, n_levels, lvl)
    lvl = np.where(s > t, -1, lvl)
    return lvl.astype(np.int32)


def _hgrn2_kernel(q_ref, f_ref, i_ref, g_ref, lb_ref, on_ref, lvl_ref, tri_ref, o_ref,
                  state_ref, b_ref, *, c, n_heads, dk):
    n_levels = int(math.log2(c))
    hc = c // 2

    @pl.when(pl.program_id(1) == 0)
    def _():
        state_ref[...] = jnp.zeros_like(state_ref)

    lvl = lvl_ref[...]
    tri = tri_ref[...]
    row = lax.broadcasted_iota(jnp.int32, (c, dk), 0)
    nt = (((1,), (1,)), ((), ()))
    tn = (((0,), (0,)), ((), ()))

    def head(hd, carry):
        col = pl.multiple_of(hd * dk, dk)
        q_raw = q_ref[0, :, pl.ds(col, dk)].astype(F32)
        f_raw = f_ref[0, :, pl.ds(col, dk)].astype(F32)
        v = i_ref[0, :, pl.ds(col, dk)]
        g_raw = g_ref[0, :, pl.ds(col, dk)].astype(F32)
        lb = lb_ref[:, pl.ds(col, dk)]

        c1 = 0.5 - 0.5 * lb
        c1t = c1 * jnp.tanh(0.5 * f_raw)
        fgate = (lb + c1) + c1t
        k = c1 - c1t
        x = jnp.log2(fgate)
        q = _silu(q_raw)

        x_hi = x.astype(BF16)
        x_lo = (x - x_hi.astype(F32)).astype(BF16)
        b2 = jnp.dot(tri, jnp.concatenate([x_hi, x_lo], axis=1), preferred_element_type=F32)
        b = b2[:, :dk] + b2[:, dk:]
        b_ref[...] = b

        q_bf = q.astype(BF16)
        k_bf = k.astype(BF16)

        def diag_blocks(dense):
            return [dense[r:r + hc, r:r + hc].astype(BF16) for r in (0, hc)]

        plain = diag_blocks(lax.dot_general(q_bf, k_bf, nt, preferred_element_type=F32))
        halves = [jnp.where(lvl == n_levels - 1, blk_, jnp.zeros_like(blk_)) for blk_ in plain]
        cross = None
        for level in range(n_levels):
            blk = c >> level
            half = blk // 2
            if blk >= 8:
                refs = [jnp.broadcast_to(b_ref[pl.ds(m * blk + half - 1, 1), :], (blk, dk))
                        for m in range(c // blk)]
                ref = refs[0] if len(refs) == 1 else jnp.concatenate(refs, axis=0)
                gl = jnp.exp2(-jnp.abs(b - ref))
            elif blk == 4:
                p = row & 3
                f_next = pltpu.roll(fgate, c - 1, 0)
                f_prev = pltpu.roll(fgate, 1, 0)
                gl = jnp.where(p == 0, f_next,
                               jnp.where(p == 1, 1.0,
                                         jnp.where(p == 2, fgate, fgate * f_prev)))
            else:
                gl = jnp.where((row & 1) == 1, fgate, 1.0)
            gl = gl.astype(BF16)
            if level == 0:
                cross = lax.dot_general(q_bf[hc:] * gl[hc:], k_bf[:hc] * gl[:hc], nt,
                                        preferred_element_type=F32).astype(BF16)
            else:
                sl = diag_blocks(lax.dot_general(q_bf * gl, k_bf * gl, nt,
                                                 preferred_element_type=F32))
                halves = [jnp.where(lvl == level - 1, s_, h_) for s_, h_ in zip(sl, halves)]

        o_intra = jnp.concatenate([
            jnp.dot(halves[0], v[:hc], preferred_element_type=F32),
            jnp.dot(jnp.concatenate([cross, halves[1]], axis=1), v, preferred_element_type=F32),
        ], axis=0)

        state = state_ref[hd]
        decay_in = jnp.exp2(b)
        o_inter = jnp.dot((q * decay_in).astype(BF16), state.astype(BF16),
                          preferred_element_type=F32)
        b_last = b_ref[pl.ds(c - 1, 1), :]
        k_out = (k * jnp.exp2(b_last - b)).astype(BF16)
        kv = lax.dot_general(k_out, v, tn, preferred_element_type=F32)
        eye = (lax.broadcasted_iota(jnp.int32, (dk, dk), 0)
               == lax.broadcasted_iota(jnp.int32, (dk, dk), 1))
        scale_col = jnp.sum(jnp.where(eye, jnp.exp2(b_last), 0.0), axis=1, keepdims=True)
        state_ref[hd] = state * scale_col + kv

        o = o_inter + o_intra
        o = o * lax.rsqrt(jnp.mean(o * o, axis=-1, keepdims=True) + RMS_EPS) * on_ref[...]
        o_ref[0, :, pl.ds(col, dk)] = (o * _silu(g_raw)).astype(o_ref.dtype)
        return carry

    lax.fori_loop(0, n_heads, head, 0, unroll=4)


def hgrn2(proj, lower_bound, out_norm, *, n_heads, dk, c):
    bsz, s_len, _ = proj.shape
    w = n_heads * dk
    assert dk == LANES and s_len % c == 0 and (c & (c - 1)) == 0 and c >= 16
    lvl = jnp.asarray(_level_ids(c // 2), BF16)
    tri = jnp.asarray(np.tril(np.ones((c, c), np.float32)), BF16)
    return pl.pallas_call(
        functools.partial(_hgrn2_kernel, c=c, n_heads=n_heads, dk=dk),
        grid=(bsz, s_len // c),
        in_specs=[
            pl.BlockSpec((1, c, w), lambda b, i: (b, i, 0)),
            pl.BlockSpec((1, c, w), lambda b, i: (b, i, 1)),
            pl.BlockSpec((1, c, w), lambda b, i: (b, i, 2)),
            pl.BlockSpec((1, c, w), lambda b, i: (b, i, 3)),
            pl.BlockSpec((1, w), lambda b, i: (0, 0)),
            pl.BlockSpec((1, dk), lambda b, i: (0, 0)),
            pl.BlockSpec((c // 2, c // 2), lambda b, i: (0, 0)),
            pl.BlockSpec((c, c), lambda b, i: (0, 0)),
        ],
        out_specs=pl.BlockSpec((1, c, w), lambda b, i: (b, i, 0)),
        out_shape=jax.ShapeDtypeStruct((bsz, s_len, w), BF16),
        scratch_shapes=[pltpu.VMEM((n_heads, dk, dk), F32), pltpu.VMEM((c, dk), F32)],
        compiler_params=_cparams(("parallel", "arbitrary")),
        name="hgrn2",
    )(proj, proj, proj, proj, lower_bound, out_norm, lvl, tri)


def _tiles(n_tokens, d_ff):
    tm = min(1024, n_tokens)
    tf = 256 if d_ff % 256 == 0 else d_ff
    return tm, tf


def trunk(x, ffn_pre_norm, ffn_pre_w_gate, ffn_pre_w_up, ffn_pre_w_down, mix_norm,
          ffn_post_norm, ffn_post_w_gate, ffn_post_w_up, ffn_post_w_down,
          ab_w_in, ab_conv_w, ab_w_out, c_w_in, c_lower_bounds, c_out_norm, c_w_out,
          final_norm, *, sb_heads, sb_head_dim, c_heads, c_head_dim, sb_tq, c_chunk):
    bsz, s_len, d = x.shape
    depth = ffn_pre_norm.shape[0]
    n = bsz * s_len
    d_ff = ffn_pre_w_gate.shape[-1]
    tm, tf = _tiles(n, d_ff)
    tm_seq = min(tm, s_len)
    bf = lambda t: t.astype(BF16)
    row = lambda t: t.reshape(1, -1).astype(F32)

    lb_soft = jax.nn.softmax(c_lower_bounds.astype(F32), axis=0)
    lb_cum = jnp.cumsum(lb_soft, axis=0)
    lower_bounds = lb_cum - lb_cum[0:1]

    a_width = ab_w_in.shape[-1] // 6
    h = x.reshape(n, d)
    for layer in range(depth):
        h = ffn(h, row(ffn_pre_norm[layer]), bf(ffn_pre_w_gate[layer]), bf(ffn_pre_w_up[layer]),
                bf(ffn_pre_w_down[layer]), row(final_norm), final_norm=False, tm=tm, tf=tf)
        mix = None
        if layer % 2 == 0:
            e = layer // 2
            proj = norm_proj(h, row(mix_norm[layer]), bf(ab_w_in[e]), tm=tm, tn=512)
            proj = proj.reshape(bsz, s_len, -1)
            y_b = sb_attention(proj, n_heads=sb_heads, dh=sb_head_dim, q_col=3 * a_width,
                               k_col=3 * a_width + sb_heads * sb_head_dim,
                               v_col=3 * a_width + 2 * sb_heads * sb_head_dim, tq=sb_tq)
            w_out = bf(ab_w_out[e])
            h = conv_out(proj, y_b, ab_conv_w[e].astype(F32), w_out[:a_width], w_out[a_width:],
                         h.reshape(bsz, s_len, d), a_width=a_width, tm=tm_seq).reshape(n, d)
        else:
            o = layer // 2
            proj = norm_proj(h, row(mix_norm[layer]), bf(c_w_in[o]), tm=tm, tn=512)
            proj = proj.reshape(bsz, s_len, -1)
            y = hgrn2(proj, row(lower_bounds[layer]), row(c_out_norm[o]),
                      n_heads=c_heads, dk=c_head_dim, c=c_chunk)
            mix = (y.reshape(n, -1), bf(c_w_out[o]))
        h = ffn(h, row(ffn_post_norm[layer]), bf(ffn_post_w_gate[layer]), bf(ffn_post_w_up[layer]),
                bf(ffn_post_w_down[layer]), row(final_norm),
                final_norm=(layer == depth - 1), tm=tm, tf=tf, mix=mix)
    return h.reshape(bsz, s_len, d)


def kernel(x, ffn_pre_norm, ffn_pre_w_gate, ffn_pre_w_up, ffn_pre_w_down, mix_norm, ffn_post_norm,
           ffn_post_w_gate, ffn_post_w_up, ffn_post_w_down, ab_w_in, ab_conv_w, ab_w_out, c_w_in,
           c_lower_bounds, c_out_norm, c_w_out, final_norm):
    return trunk(x, ffn_pre_norm, ffn_pre_w_gate, ffn_pre_w_up, ffn_pre_w_down, mix_norm,
                 ffn_post_norm, ffn_post_w_gate, ffn_post_w_up, ffn_post_w_down,
                 ab_w_in, ab_conv_w, ab_w_out, c_w_in, c_lower_bounds, c_out_norm, c_w_out,
                 final_norm, sb_heads=8, sb_head_dim=64, c_heads=8, c_head_dim=128,
                 sb_tq=256, c_chunk=256)
```

```python
import functools
import math

import numpy as np
import jax
import jax.numpy as jnp
from jax import lax
from jax.experimental import pallas as pl
from jax.experimental.pallas import tpu as pltpu

RMS_EPS = 1e-6
MACARON_WEIGHT = 0.5
LANES = 128
VMEM_LIMIT_BYTES = 56 * 1024 * 1024
SB_EXIT_LOGDECAY = 104.0
LOG2E = 1.4426950408889634

F32 = jnp.float32
BF16 = jnp.bfloat16


def _cparams(sem):
    return pltpu.CompilerParams(dimension_semantics=sem, vmem_limit_bytes=VMEM_LIMIT_BYTES)


def _rms_norm_f32(x, gain):
    return x * lax.rsqrt(jnp.mean(x * x, axis=-1, keepdims=True) + RMS_EPS) * gain


def _sigmoid(x):
    return 1.0 / (1.0 + jnp.exp(-x))


def _silu(x):
    hx = 0.5 * x
    return hx + hx * jnp.tanh(hx)


def _mix_ffn_kernel(y_ref, wm_ref, h_ref, gain_ref, wg_ref, wu_ref, wd_ref, fgain_ref, o_ref,
                    xn_ref, acc_ref, *, final_norm, tf):
    o_ref[...] = h_ref[...] + jnp.dot(y_ref[...], wm_ref[...], preferred_element_type=F32)
    _ffn_kernel(o_ref, gain_ref, wg_ref, wu_ref, wd_ref, fgain_ref, o_ref, xn_ref, acc_ref,
                final_norm=final_norm, tf=tf)


def _ffn_kernel(h_ref, gain_ref, wg_ref, wu_ref, wd_ref, fgain_ref, o_ref, xn_ref, acc_ref,
                *, final_norm, tf):
    xn_ref[...] = _rms_norm_f32(h_ref[...], gain_ref[...]).astype(BF16)
    acc_ref[...] = jnp.zeros_like(acc_ref)

    def chunk(j, carry):
        col = pl.multiple_of(j * tf, tf)
        xn = xn_ref[...]
        g = jnp.dot(xn, wg_ref[:, pl.ds(col, tf)], preferred_element_type=F32)
        u = jnp.dot(xn, wu_ref[:, pl.ds(col, tf)], preferred_element_type=F32)
        a = (g * _sigmoid(g) * u).astype(BF16)
        acc_ref[...] += jnp.dot(a, wd_ref[pl.ds(col, tf), :], preferred_element_type=F32)
        return carry

    lax.fori_loop(0, wg_ref.shape[1] // tf, chunk, 0, unroll=True)
    out = h_ref[...] + MACARON_WEIGHT * acc_ref[...]
    if final_norm:
        out = _rms_norm_f32(out, fgain_ref[...])
    o_ref[...] = out


def ffn(h, gain, wg, wu, wd, fgain, *, final_norm, tm, tf, mix=None):
    n, d = h.shape
    f = wg.shape[1]
    resident = lambda shape: pl.BlockSpec(shape, lambda i: (0, 0), pipeline_mode=pl.Buffered(1))
    tile = lambda width: pl.BlockSpec((tm, width), lambda i: (i, 0))
    in_specs = [tile(d), resident((1, d)), resident((d, f)), resident((d, f)), resident((f, d)),
                resident((1, d))]
    args = (h, gain, wg, wu, wd, fgain)
    body = _ffn_kernel
    if mix is not None:
        y, wm = mix
        in_specs = [tile(y.shape[1]), resident(wm.shape)] + in_specs
        args = (y, wm) + args
        body = _mix_ffn_kernel
    return pl.pallas_call(
        functools.partial(body, final_norm=final_norm, tf=tf),
        grid=(n // tm,),
        in_specs=in_specs,
        out_specs=tile(d),
        out_shape=jax.ShapeDtypeStruct((n, d), F32),
        scratch_shapes=[pltpu.VMEM((tm, d), BF16), pltpu.VMEM((tm, d), F32)],
        compiler_params=_cparams(("parallel",)),
        name="ffn" if mix is None else "mix_ffn",
    )(*args)


def _norm_proj_kernel(h_ref, gain_ref, w_ref, o_ref, xn_ref, *, tn):
    xn_ref[...] = _rms_norm_f32(h_ref[...], gain_ref[...]).astype(BF16)

    def chunk(j, carry):
        col = pl.multiple_of(j * tn, tn)
        o_ref[:, pl.ds(col, tn)] = jnp.dot(xn_ref[...], w_ref[:, pl.ds(col, tn)],
                                           preferred_element_type=F32).astype(o_ref.dtype)
        return carry

    lax.fori_loop(0, w_ref.shape[1] // tn, chunk, 0, unroll=True)


def norm_proj(h, gain, w, *, tm, tn):
    n, d = h.shape
    c = w.shape[1]
    resident = lambda shape: pl.BlockSpec(shape, lambda i: (0, 0), pipeline_mode=pl.Buffered(1))
    return pl.pallas_call(
        functools.partial(_norm_proj_kernel, tn=tn),
        grid=(n // tm,),
        in_specs=[pl.BlockSpec((tm, d), lambda i: (i, 0)), resident((1, d)), resident((d, c))],
        out_specs=pl.BlockSpec((tm, c), lambda i: (i, 0)),
        out_shape=jax.ShapeDtypeStruct((n, c), BF16),
        scratch_shapes=[pltpu.VMEM((tm, d), BF16)],
        compiler_params=_cparams(("parallel",)),
        name="norm_proj",
    )(h, gain, w)


def _sb_attn_kernel(q_ref, k_ref, v_ref, tri_ref, o_ref, acc_ref, dec_ref, *, tq, dh, scale, n_q):
    lane = lax.broadcasted_iota(jnp.int32, (tq, LANES), 1)
    head_b = lane >= dh
    tri = tri_ref[...]
    row = lax.broadcasted_iota(jnp.int32, (tq, tq), 0)
    col = lax.broadcasted_iota(jnp.int32, (tq, tq), 1)
    causal = col < row
    nt = (((1,), (1,)), ((), ()))

    def key_block(j):
        start = pl.multiple_of(j * tq, tq)
        return k_ref[0, pl.ds(start, tq), :], v_ref[0, pl.ds(start, tq), :]

    def head_block(qh, kb, vb, dec, diagonal):
        z = lax.dot_general(qh, kb, nt, preferred_element_type=F32)
        sp = jnp.maximum(z, 0.0) + jnp.log2(1.0 + jnp.exp2(-jnp.abs(z)))
        if diagonal:
            sp = jnp.where(causal, sp, 0.0)
        later = jnp.dot(sp.astype(BF16), tri, preferred_element_type=F32)
        arg = z - later
        total = later[:, 0:1]
        if dec is not None:
            arg = arg - dec
            total = total + dec
        w = jnp.exp2(arg)
        if diagonal:
            w = jnp.where(causal, w, 0.0)
        return jnp.dot(w.astype(BF16), vb, preferred_element_type=F32), total

    def q_heads_of(i):
        qs = pl.multiple_of(i * tq, tq)
        q = (q_ref[0, pl.ds(qs, tq), :].astype(F32) * scale).astype(BF16)
        zero = jnp.zeros_like(q)
        return jnp.where(head_b, zero, q), jnp.where(head_b, q, zero)

    def head_blocks(i, n_static):
        q_heads = q_heads_of(i)
        blocks = [key_block(i - d) for d in range(n_static)]
        outs = []
        for hh in range(2):
            out, dec = head_block(q_heads[hh], *blocks[0], None, True)
            for kb, vb in blocks[1:]:
                more, dec = head_block(q_heads[hh], kb, vb, dec, False)
                out = out + more
            outs.append(out)
            dec_ref[hh] = dec
        acc_ref[...] = jnp.where(head_b, outs[1], outs[0])

    def finish(i, n_static):
        def undecayed():
            return jnp.min(dec_ref[...]) < SB_EXIT_LOGDECAY * LOG2E

        def cond(state):
            j, go = state
            return jnp.logical_and(j >= 0, go)

        def body(state):
            j, _ = state
            q_heads = q_heads_of(i)
            kb, vb = key_block(j)
            more = []
            for hh in range(2):
                out, dec = head_block(q_heads[hh], kb, vb, dec_ref[hh], False)
                dec_ref[hh] = dec
                more.append(out)
            acc_ref[...] += jnp.where(head_b, more[1], more[0])
            return j - 1, undecayed()

        lax.while_loop(cond, body, (i - n_static, undecayed()))
        qs = pl.multiple_of(i * tq, tq)
        o_ref[0, pl.ds(qs, tq), :] = acc_ref[...].astype(o_ref.dtype)

    first = jnp.int32(0)
    head_blocks(first, 1)
    finish(first, 1)

    def rest(i, carry):
        head_blocks(i, 2)
        finish(i, 2)
        return carry

    lax.fori_loop(1, n_q, rest, 0)


def sb_attention(proj, *, n_heads, dh, q_col, k_col, v_col, tq):
    bsz, s_len, _ = proj.shape
    assert 2 * dh == LANES and n_heads % 2 == 0
    assert q_col % LANES == 0 and k_col % LANES == 0 and v_col % LANES == 0
    pairs = n_heads // 2
    qb, kb, vb = q_col // LANES, k_col // LANES, v_col // LANES
    tri = jnp.asarray(np.tril(np.ones((tq, tq), np.float32)), BF16)
    seq = lambda c: pl.BlockSpec((1, s_len, LANES), lambda b, p: (b, 0, c + p))
    return pl.pallas_call(
        functools.partial(_sb_attn_kernel, tq=tq, dh=dh, scale=LOG2E / math.sqrt(dh),
                          n_q=s_len // tq),
        grid=(bsz, pairs),
        in_specs=[seq(qb), seq(kb), seq(vb), pl.BlockSpec((tq, tq), lambda b, p: (0, 0))],
        out_specs=seq(0),
        out_shape=jax.ShapeDtypeStruct((bsz, s_len, pairs * LANES), BF16),
        scratch_shapes=[pltpu.VMEM((tq, LANES), F32), pltpu.VMEM((2, tq, 1), F32)],
        compiler_params=_cparams(("parallel", "parallel")),
        name="sb_attn",
    )(proj, proj, proj, tri)


def _conv_ffn_kernel(ab_ref, ac_ref, ax_ref, pc_ref, px_ref, yb_ref, cw_ref, wa_ref, wb_ref,
                     h_ref, gain_ref, wg_ref, wu_ref, wd_ref, fgain_ref, o_ref, xn_ref, acc_ref,
                     *, final_norm, tf, tiles_per_seq):
    seq_start = (pl.program_id(0) % tiles_per_seq) == 0
    u = ac_ref[...].astype(F32) * ax_ref[...].astype(F32)
    prev = pc_ref[...].astype(F32) * px_ref[...].astype(F32)
    prev = jnp.where(seq_start, 0.0, prev)
    row = lax.broadcasted_iota(jnp.int32, u.shape, 0)
    u1 = pltpu.roll(u, 1, 0)
    u1 = jnp.where(row == 0, prev[7:8, :], u1)
    u2 = pltpu.roll(u, 2, 0)
    u2 = jnp.where(row == 0, prev[6:7, :], jnp.where(row == 1, prev[7:8, :], u2))
    cw = cw_ref[...]
    conv = cw[0:1, :] * u2 + cw[1:2, :] * u1 + cw[2:3, :] * u
    y_a = (ab_ref[...].astype(F32) * conv).astype(BF16)
    o_ref[...] = (h_ref[...]
                  + jnp.dot(y_a, wa_ref[...], preferred_element_type=F32)
                  + jnp.dot(yb_ref[...], wb_ref[...], preferred_element_type=F32))
    _ffn_kernel(o_ref, gain_ref, wg_ref, wu_ref, wd_ref, fgain_ref, o_ref, xn_ref, acc_ref,
                final_norm=final_norm, tf=tf)


def conv_ffn(proj, y_b, conv_w, w_out_a, w_out_b, h, gain, wg, wu, wd, fgain, *, a_width, s_len,
             final_norm, tm, tf):
    n, d = h.shape
    f = wg.shape[1]
    assert s_len % tm == 0 and tm % 8 == 0
    halo = tm // 8
    resident = lambda shape: pl.BlockSpec(shape, lambda i: (0, 0), pipeline_mode=pl.Buffered(1))
    a_cols = lambda c: pl.BlockSpec((tm, a_width), lambda i: (i, c))
    a_halo = lambda c: pl.BlockSpec((8, a_width), lambda i: (jnp.maximum(i * halo - 1, 0), c))
    return pl.pallas_call(
        functools.partial(_conv_ffn_kernel, final_norm=final_norm, tf=tf,
                          tiles_per_seq=s_len // tm),
        grid=(n // tm,),
        in_specs=[
            a_cols(0), a_cols(1), a_cols(2), a_halo(1), a_halo(2),
            pl.BlockSpec((tm, y_b.shape[1]), lambda i: (i, 0)),
            resident(conv_w.shape), resident(w_out_a.shape), resident(w_out_b.shape),
            pl.BlockSpec((tm, d), lambda i: (i, 0)),
            resident((1, d)), resident((d, f)), resident((d, f)), resident((f, d)), resident((1, d)),
        ],
        out_specs=pl.BlockSpec((tm, d), lambda i: (i, 0)),
        out_shape=jax.ShapeDtypeStruct((n, d), F32),
        scratch_shapes=[pltpu.VMEM((tm, d), BF16), pltpu.VMEM((tm, d), F32)],
        compiler_params=_cparams(("parallel",)),
        name="conv_ffn",
    )(proj, proj, proj, proj, proj, y_b, conv_w, w_out_a, w_out_b, h, gain, wg, wu, wd, fgain)


def _level_ids(c):
    t = np.arange(c)[:, None]
    s = np.arange(c)[None, :]
    x = np.bitwise_xor(t, s)
    n_levels = int(math.log2(c))
    msb = np.floor(np.log2(np.maximum(x, 1))).astype(np.int64)
    lvl = (n_levels - 1) - msb
    lvl = np.where(x == 0, n_levels, lvl)
    lvl = np.where(s > t, -1, lvl)
    return lvl.astype(np.int32)


def _hgrn2_kernel(q_ref, f_ref, i_ref, g_ref, lb_ref, on_ref, lvl_ref, tri_ref, o_ref,
                  state_ref, b_ref, *, c, n_heads, dk):
    n_levels = int(math.log2(c))
    hc = c // 2

    @pl.when(pl.program_id(1) == 0)
    def _():
        state_ref[...] = jnp.zeros_like(state_ref)

    lvl = lvl_ref[...]
    tri = tri_ref[...]
    row = lax.broadcasted_iota(jnp.int32, (c, dk), 0)
    nt = (((1,), (1,)), ((), ()))
    tn = (((0,), (0,)), ((), ()))

    def head(hd, carry):
        col = pl.multiple_of(hd * dk, dk)
        q_raw = q_ref[0, :, pl.ds(col, dk)].astype(F32)
        f_raw = f_ref[0, :, pl.ds(col, dk)].astype(F32)
        v = i_ref[0, :, pl.ds(col, dk)]
        g_raw = g_ref[0, :, pl.ds(col, dk)].astype(F32)
        lb = lb_ref[:, pl.ds(col, dk)]

        c1 = 0.5 - 0.5 * lb
        c1t = c1 * jnp.tanh(0.5 * f_raw)
        fgate = (lb + c1) + c1t
        k = c1 - c1t
        x = jnp.log2(fgate)
        q = _silu(q_raw)

        x_hi = x.astype(BF16)
        x_lo = (x - x_hi.astype(F32)).astype(BF16)
        b2 = jnp.dot(tri, jnp.concatenate([x_hi, x_lo], axis=1), preferred_element_type=F32)
        b = b2[:, :dk] + b2[:, dk:]
        b_ref[...] = b

        q_bf = q.astype(BF16)
        k_bf = k.astype(BF16)

        def diag_blocks(dense):
            return [dense[r:r + hc, r:r + hc].astype(BF16) for r in (0, hc)]

        plain = diag_blocks(lax.dot_general(q_bf, k_bf, nt, preferred_element_type=F32))
        halves = [jnp.where(lvl == n_levels - 1, blk_, jnp.zeros_like(blk_)) for blk_ in plain]
        cross = None
        for level in range(n_levels):
            blk = c >> level
            half = blk // 2
            if blk >= 8:
                refs = [jnp.broadcast_to(b_ref[pl.ds(m * blk + half - 1, 1), :], (blk, dk))
                        for m in range(c // blk)]
                ref = refs[0] if len(refs) == 1 else jnp.concatenate(refs, axis=0)
                gl = jnp.exp2(-jnp.abs(b - ref))
            elif blk == 4:
                p = row & 3
                f_next = pltpu.roll(fgate, c - 1, 0)
                f_prev = pltpu.roll(fgate, 1, 0)
                gl = jnp.where(p == 0, f_next,
                               jnp.where(p == 1, 1.0,
                                         jnp.where(p == 2, fgate, fgate * f_prev)))
            else:
                gl = jnp.where((row & 1) == 1, fgate, 1.0)
            gl = gl.astype(BF16)
            if level == 0:
                cross = lax.dot_general(q_bf[hc:] * gl[hc:], k_bf[:hc] * gl[:hc], nt,
                                        preferred_element_type=F32).astype(BF16)
            else:
                sl = diag_blocks(lax.dot_general(q_bf * gl, k_bf * gl, nt,
                                                 preferred_element_type=F32))
                halves = [jnp.where(lvl == level - 1, s_, h_) for s_, h_ in zip(sl, halves)]

        o_intra = jnp.concatenate([
            jnp.dot(halves[0], v[:hc], preferred_element_type=F32),
            jnp.dot(jnp.concatenate([cross, halves[1]], axis=1), v, preferred_element_type=F32),
        ], axis=0)

        state = state_ref[hd]
        decay_in = jnp.exp2(b)
        o_inter = jnp.dot((q * decay_in).astype(BF16), state.astype(BF16),
                          preferred_element_type=F32)
        b_last = b_ref[pl.ds(c - 1, 1), :]
        k_out = (k * jnp.exp2(b_last - b)).astype(BF16)
        kv = lax.dot_general(k_out, v, tn, preferred_element_type=F32)
        eye = (lax.broadcasted_iota(jnp.int32, (dk, dk), 0)
               == lax.broadcasted_iota(jnp.int32, (dk, dk), 1))
        scale_col = jnp.sum(jnp.where(eye, jnp.exp2(b_last), 0.0), axis=1, keepdims=True)
        state_ref[hd] = state * scale_col + kv

        o = o_inter + o_intra
        o = o * lax.rsqrt(jnp.mean(o * o, axis=-1, keepdims=True) + RMS_EPS) * on_ref[...]
        o_ref[0, :, pl.ds(col, dk)] = (o * _silu(g_raw)).astype(o_ref.dtype)
        return carry

    lax.fori_loop(0, n_heads, head, 0, unroll=4)


def hgrn2(proj, lower_bound, out_norm, *, n_heads, dk, c):
    bsz, s_len, _ = proj.shape
    w = n_heads * dk
    assert dk == LANES and s_len % c == 0 and (c & (c - 1)) == 0 and c >= 16
    lvl = jnp.asarray(_level_ids(c // 2), BF16)
    tri = jnp.asarray(np.tril(np.ones((c, c), np.float32)), BF16)
    return pl.pallas_call(
        functools.partial(_hgrn2_kernel, c=c, n_heads=n_heads, dk=dk),
        grid=(bsz, s_len // c),
        in_specs=[
            pl.BlockSpec((1, c, w), lambda b, i: (b, i, 0)),
            pl.BlockSpec((1, c, w), lambda b, i: (b, i, 1)),
            pl.BlockSpec((1, c, w), lambda b, i: (b, i, 2)),
            pl.BlockSpec((1, c, w), lambda b, i: (b, i, 3)),
            pl.BlockSpec((1, w), lambda b, i: (0, 0)),
            pl.BlockSpec((1, dk), lambda b, i: (0, 0)),
            pl.BlockSpec((c // 2, c // 2), lambda b, i: (0, 0)),
            pl.BlockSpec((c, c), lambda b, i: (0, 0)),
        ],
        out_specs=pl.BlockSpec((1, c, w), lambda b, i: (b, i, 0)),
        out_shape=jax.ShapeDtypeStruct((bsz, s_len, w), BF16),
        scratch_shapes=[pltpu.VMEM((n_heads, dk, dk), F32), pltpu.VMEM((c, dk), F32)],
        compiler_params=_cparams(("parallel", "arbitrary")),
        name="hgrn2",
    )(proj, proj, proj, proj, lower_bound, out_norm, lvl, tri)


def _tiles(n_tokens, d_ff):
    tm = min(1024, n_tokens)
    tf = 256 if d_ff % 256 == 0 else d_ff
    return tm, tf


def trunk(x, ffn_pre_norm, ffn_pre_w_gate, ffn_pre_w_up, ffn_pre_w_down, mix_norm,
          ffn_post_norm, ffn_post_w_gate, ffn_post_w_up, ffn_post_w_down,
          ab_w_in, ab_conv_w, ab_w_out, c_w_in, c_lower_bounds, c_out_norm, c_w_out,
          final_norm, *, sb_heads, sb_head_dim, c_heads, c_head_dim, sb_tq, c_chunk):
    bsz, s_len, d = x.shape
    depth = ffn_pre_norm.shape[0]
    n = bsz * s_len
    d_ff = ffn_pre_w_gate.shape[-1]
    tm, tf = _tiles(n, d_ff)
    tm_seq = min(tm, s_len)
    bf = lambda t: t.astype(BF16)
    row = lambda t: t.reshape(1, -1).astype(F32)

    lb_soft = jax.nn.softmax(c_lower_bounds.astype(F32), axis=0)
    lb_cum = jnp.cumsum(lb_soft, axis=0)
    lower_bounds = lb_cum - lb_cum[0:1]

    a_width = ab_w_in.shape[-1] // 6
    h = x.reshape(n, d)
    for layer in range(depth):
        h = ffn(h, row(ffn_pre_norm[layer]), bf(ffn_pre_w_gate[layer]), bf(ffn_pre_w_up[layer]),
                bf(ffn_pre_w_down[layer]), row(final_norm), final_norm=False, tm=tm, tf=tf)
        post = (row(ffn_post_norm[layer]), bf(ffn_post_w_gate[layer]), bf(ffn_post_w_up[layer]),
                bf(ffn_post_w_down[layer]), row(final_norm))
        last = layer == depth - 1
        if layer % 2 == 0:
            e = layer // 2
            proj = norm_proj(h, row(mix_norm[layer]), bf(ab_w_in[e]), tm=tm, tn=512)
            y_b = sb_attention(proj.reshape(bsz, s_len, -1), n_heads=sb_heads, dh=sb_head_dim,
                               q_col=3 * a_width, k_col=3 * a_width + sb_heads * sb_head_dim,
                               v_col=3 * a_width + 2 * sb_heads * sb_head_dim, tq=sb_tq)
            w_out = bf(ab_w_out[e])
            h = conv_ffn(proj, y_b.reshape(n, -1), ab_conv_w[e].astype(F32), w_out[:a_width],
                         w_out[a_width:], h, *post, a_width=a_width, s_len=s_len,
                         final_norm=last, tm=tm_seq, tf=tf)
        else:
            o = layer // 2
            proj = norm_proj(h, row(mix_norm[layer]), bf(c_w_in[o]), tm=tm, tn=512)
            y = hgrn2(proj.reshape(bsz, s_len, -1), row(lower_bounds[layer]), row(c_out_norm[o]),
                      n_heads=c_heads, dk=c_head_dim, c=c_chunk)
            h = ffn(h, *post, final_norm=last, tm=tm, tf=tf,
                    mix=(y.reshape(n, -1), bf(c_w_out[o])))
    return h.reshape(bsz, s_len, d)


def kernel(x, ffn_pre_norm, ffn_pre_w_gate, ffn_pre_w_up, ffn_pre_w_down, mix_norm, ffn_post_norm,
           ffn_post_w_gate, ffn_post_w_up, ffn_post_w_down, ab_w_in, ab_conv_w, ab_w_out, c_w_in,
           c_lower_bounds, c_out_norm, c_w_out, final_norm):
    return trunk(x, ffn_pre_norm, ffn_pre_w_gate, ffn_pre_w_up, ffn_pre_w_down, mix_norm,
                 ffn_post_norm, ffn_post_w_gate, ffn_post_w_up, ffn_post_w_down,
                 ab_w_in, ab_conv_w, ab_w_out, c_w_in, c_lower_bounds, c_out_norm, c_w_out,
                 final_norm, sb_heads=8, sb_head_dim=64, c_heads=8, c_head_dim=128,
                 sb_tq=256, c_chunk=256)
```

```python
import functools
import math

import numpy as np
import jax
import jax.numpy as jnp
from jax import lax
from jax.experimental import pallas as pl
from jax.experimental.pallas import tpu as pltpu

RMS_EPS = 1e-6
MACARON_WEIGHT = 0.5
LANES = 128
VMEM_LIMIT_BYTES = 56 * 1024 * 1024
SB_EXIT_LOGDECAY = 104.0
LOG2E = 1.4426950408889634

F32 = jnp.float32
BF16 = jnp.bfloat16


def _cparams(sem):
    return pltpu.CompilerParams(dimension_semantics=sem, vmem_limit_bytes=VMEM_LIMIT_BYTES)


def _resident(shape):
    return pl.BlockSpec(shape, lambda i: (0,) * len(shape), pipeline_mode=pl.Buffered(1))


def _resident_layer(stacked, layer):
    rest = stacked.shape[1:]
    return pl.BlockSpec((None,) + rest, lambda i: (layer,) + (0,) * len(rest),
                        pipeline_mode=pl.Buffered(1))


def _rms_norm_f32(x, gain):
    return x * lax.rsqrt(jnp.mean(x * x, axis=-1, keepdims=True) + RMS_EPS) * gain


def _sigmoid(x):
    return 1.0 / (1.0 + jnp.exp(-x))


def _silu(x):
    hx = 0.5 * x
    return hx + hx * jnp.tanh(hx)


def _mix_ffn_kernel(y_ref, wm_ref, h_ref, gain_ref, wg_ref, wu_ref, wd_ref, fgain_ref, o_ref,
                    xn_ref, acc_ref, *, final_norm, tf):
    o_ref[...] = h_ref[...] + jnp.dot(y_ref[...], wm_ref[...], preferred_element_type=F32)
    _ffn_kernel(o_ref, gain_ref, wg_ref, wu_ref, wd_ref, fgain_ref, o_ref, xn_ref, acc_ref,
                final_norm=final_norm, tf=tf)


def _ffn_kernel(h_ref, gain_ref, wg_ref, wu_ref, wd_ref, fgain_ref, o_ref, xn_ref, acc_ref,
                *, final_norm, tf):
    xn_ref[...] = _rms_norm_f32(h_ref[...], gain_ref[...]).astype(BF16)
    acc_ref[...] = jnp.zeros_like(acc_ref)

    def chunk(j, carry):
        col = pl.multiple_of(j * tf, tf)
        xn = xn_ref[...]
        g = jnp.dot(xn, wg_ref[:, pl.ds(col, tf)], preferred_element_type=F32)
        u = jnp.dot(xn, wu_ref[:, pl.ds(col, tf)], preferred_element_type=F32)
        a = (g * _sigmoid(g) * u).astype(BF16)
        acc_ref[...] += jnp.dot(a, wd_ref[pl.ds(col, tf), :], preferred_element_type=F32)
        return carry

    lax.fori_loop(0, wg_ref.shape[1] // tf, chunk, 0, unroll=True)
    out = h_ref[...] + MACARON_WEIGHT * acc_ref[...]
    if final_norm:
        out = _rms_norm_f32(out, fgain_ref[...])
    o_ref[...] = out


def ffn(h, gain, wg, wu, wd, fgain, *, layer, final_norm, tm, tf, mix=None):
    n, d = h.shape
    tile = lambda width: pl.BlockSpec((tm, width), lambda i: (i, 0))
    in_specs = [tile(d), _resident((1, d)), _resident_layer(wg, layer), _resident_layer(wu, layer),
                _resident_layer(wd, layer), _resident((1, d))]
    args = (h, gain, wg, wu, wd, fgain)
    body = _ffn_kernel
    if mix is not None:
        y, wm = mix
        in_specs = [tile(y.shape[1]), _resident(wm.shape)] + in_specs
        args = (y, wm) + args
        body = _mix_ffn_kernel
    return pl.pallas_call(
        functools.partial(body, final_norm=final_norm, tf=tf),
        grid=(n // tm,),
        in_specs=in_specs,
        out_specs=tile(d),
        out_shape=jax.ShapeDtypeStruct((n, d), F32),
        scratch_shapes=[pltpu.VMEM((tm, d), BF16), pltpu.VMEM((tm, d), F32)],
        compiler_params=_cparams(("parallel",)),
        name="ffn" if mix is None else "mix_ffn",
    )(*args)


def _norm_proj_kernel(h_ref, gain_ref, w_ref, o_ref, xn_ref, *, tn):
    xn_ref[...] = _rms_norm_f32(h_ref[...], gain_ref[...]).astype(BF16)

    def chunk(j, carry):
        col = pl.multiple_of(j * tn, tn)
        o_ref[:, pl.ds(col, tn)] = jnp.dot(xn_ref[...], w_ref[:, pl.ds(col, tn)],
                                           preferred_element_type=F32).astype(o_ref.dtype)
        return carry

    lax.fori_loop(0, w_ref.shape[1] // tn, chunk, 0, unroll=True)


def norm_proj(h, gain, w, *, tm, tn):
    n, d = h.shape
    c = w.shape[1]
    resident = lambda shape: pl.BlockSpec(shape, lambda i: (0, 0), pipeline_mode=pl.Buffered(1))
    return pl.pallas_call(
        functools.partial(_norm_proj_kernel, tn=tn),
        grid=(n // tm,),
        in_specs=[pl.BlockSpec((tm, d), lambda i: (i, 0)), resident((1, d)), resident((d, c))],
        out_specs=pl.BlockSpec((tm, c), lambda i: (i, 0)),
        out_shape=jax.ShapeDtypeStruct((n, c), BF16),
        scratch_shapes=[pltpu.VMEM((tm, d), BF16)],
        compiler_params=_cparams(("parallel",)),
        name="norm_proj",
    )(h, gain, w)


def _sb_attn_kernel(q_ref, k_ref, v_ref, tri_ref, o_ref, acc_ref, dec_ref, *, tq, dh, scale, n_q):
    lane = lax.broadcasted_iota(jnp.int32, (tq, LANES), 1)
    head_b = lane >= dh
    tri = tri_ref[...]
    row = lax.broadcasted_iota(jnp.int32, (tq, tq), 0)
    col = lax.broadcasted_iota(jnp.int32, (tq, tq), 1)
    causal = col < row
    nt = (((1,), (1,)), ((), ()))

    def key_block(j):
        start = pl.multiple_of(j * tq, tq)
        return k_ref[0, pl.ds(start, tq), :], v_ref[0, pl.ds(start, tq), :]

    def head_block(qh, kb, vb, dec, diagonal):
        z = lax.dot_general(qh, kb, nt, preferred_element_type=F32)
        sp = jnp.maximum(z, 0.0) + jnp.log2(1.0 + jnp.exp2(-jnp.abs(z)))
        if diagonal:
            sp = jnp.where(causal, sp, 0.0)
        later = jnp.dot(sp.astype(BF16), tri, preferred_element_type=F32)
        arg = z - later
        total = later[:, 0:1]
        if dec is not None:
            arg = arg - dec
            total = total + dec
        w = jnp.exp2(arg)
        if diagonal:
            w = jnp.where(causal, w, 0.0)
        return jnp.dot(w.astype(BF16), vb, preferred_element_type=F32), total

    def q_heads_of(i):
        qs = pl.multiple_of(i * tq, tq)
        q = (q_ref[0, pl.ds(qs, tq), :].astype(F32) * scale).astype(BF16)
        zero = jnp.zeros_like(q)
        return jnp.where(head_b, zero, q), jnp.where(head_b, q, zero)

    def head_blocks(i, n_static):
        q_heads = q_heads_of(i)
        blocks = [key_block(i - d) for d in range(n_static)]
        outs = []
        for hh in range(2):
            out, dec = head_block(q_heads[hh], *blocks[0], None, True)
            for kb, vb in blocks[1:]:
                more, dec = head_block(q_heads[hh], kb, vb, dec, False)
                out = out + more
            outs.append(out)
            dec_ref[hh] = dec
        acc_ref[...] = jnp.where(head_b, outs[1], outs[0])

    def finish(i, n_static):
        def undecayed():
            return jnp.min(dec_ref[...]) < SB_EXIT_LOGDECAY * LOG2E

        def cond(state):
            j, go = state
            return jnp.logical_and(j >= 0, go)

        def body(state):
            j, _ = state
            q_heads = q_heads_of(i)
            kb, vb = key_block(j)
            more = []
            for hh in range(2):
                out, dec = head_block(q_heads[hh], kb, vb, dec_ref[hh], False)
                dec_ref[hh] = dec
                more.append(out)
            acc_ref[...] += jnp.where(head_b, more[1], more[0])
            return j - 1, undecayed()

        lax.while_loop(cond, body, (i - n_static, undecayed()))
        qs = pl.multiple_of(i * tq, tq)
        o_ref[0, pl.ds(qs, tq), :] = acc_ref[...].astype(o_ref.dtype)

    first = jnp.int32(0)
    head_blocks(first, 1)
    finish(first, 1)

    def rest(i, carry):
        head_blocks(i, 2)
        finish(i, 2)
        return carry

    lax.fori_loop(1, n_q, rest, 0)


def sb_attention(proj, *, n_heads, dh, q_col, k_col, v_col, tq):
    bsz, s_len, _ = proj.shape
    assert 2 * dh == LANES and n_heads % 2 == 0
    assert q_col % LANES == 0 and k_col % LANES == 0 and v_col % LANES == 0
    pairs = n_heads // 2
    qb, kb, vb = q_col // LANES, k_col // LANES, v_col // LANES
    tri = jnp.asarray(np.tril(np.ones((tq, tq), np.float32)), BF16)
    seq = lambda c: pl.BlockSpec((1, s_len, LANES), lambda b, p: (b, 0, c + p))
    return pl.pallas_call(
        functools.partial(_sb_attn_kernel, tq=tq, dh=dh, scale=LOG2E / math.sqrt(dh),
                          n_q=s_len // tq),
        grid=(bsz, pairs),
        in_specs=[seq(qb), seq(kb), seq(vb), pl.BlockSpec((tq, tq), lambda b, p: (0, 0))],
        out_specs=seq(0),
        out_shape=jax.ShapeDtypeStruct((bsz, s_len, pairs * LANES), BF16),
        scratch_shapes=[pltpu.VMEM((tq, LANES), F32), pltpu.VMEM((2, tq, 1), F32)],
        compiler_params=_cparams(("parallel", "parallel")),
        name="sb_attn",
    )(proj, proj, proj, tri)


def _conv_ffn_kernel(ab_ref, ac_ref, ax_ref, pc_ref, px_ref, yb_ref, cw_ref, wa_ref, wb_ref,
                     h_ref, gain_ref, wg_ref, wu_ref, wd_ref, fgain_ref, o_ref, xn_ref, acc_ref,
                     *, final_norm, tf, tiles_per_seq):
    seq_start = (pl.program_id(0) % tiles_per_seq) == 0
    u = ac_ref[...].astype(F32) * ax_ref[...].astype(F32)
    prev = pc_ref[...].astype(F32) * px_ref[...].astype(F32)
    prev = jnp.where(seq_start, 0.0, prev)
    row = lax.broadcasted_iota(jnp.int32, u.shape, 0)
    u1 = pltpu.roll(u, 1, 0)
    u1 = jnp.where(row == 0, prev[7:8, :], u1)
    u2 = pltpu.roll(u, 2, 0)
    u2 = jnp.where(row == 0, prev[6:7, :], jnp.where(row == 1, prev[7:8, :], u2))
    cw = cw_ref[...]
    conv = cw[0:1, :] * u2 + cw[1:2, :] * u1 + cw[2:3, :] * u
    y_a = (ab_ref[...].astype(F32) * conv).astype(BF16)
    o_ref[...] = (h_ref[...]
                  + jnp.dot(y_a, wa_ref[...], preferred_element_type=F32)
                  + jnp.dot(yb_ref[...], wb_ref[...], preferred_element_type=F32))
    _ffn_kernel(o_ref, gain_ref, wg_ref, wu_ref, wd_ref, fgain_ref, o_ref, xn_ref, acc_ref,
                final_norm=final_norm, tf=tf)


def conv_ffn(proj, y_b, conv_w, w_out_a, w_out_b, h, gain, wg, wu, wd, fgain, *, a_width, s_len,
             layer, final_norm, tm, tf):
    n, d = h.shape
    assert s_len % tm == 0 and tm % 8 == 0
    halo = tm // 8
    a_cols = lambda c: pl.BlockSpec((tm, a_width), lambda i: (i, c))
    a_halo = lambda c: pl.BlockSpec((8, a_width), lambda i: (jnp.maximum(i * halo - 1, 0), c))
    return pl.pallas_call(
        functools.partial(_conv_ffn_kernel, final_norm=final_norm, tf=tf,
                          tiles_per_seq=s_len // tm),
        grid=(n // tm,),
        in_specs=[
            a_cols(0), a_cols(1), a_cols(2), a_halo(1), a_halo(2),
            pl.BlockSpec((tm, y_b.shape[1]), lambda i: (i, 0)),
            _resident(conv_w.shape), _resident(w_out_a.shape), _resident(w_out_b.shape),
            pl.BlockSpec((tm, d), lambda i: (i, 0)),
            _resident((1, d)), _resident_layer(wg, layer), _resident_layer(wu, layer),
            _resident_layer(wd, layer), _resident((1, d)),
        ],
        out_specs=pl.BlockSpec((tm, d), lambda i: (i, 0)),
        out_shape=jax.ShapeDtypeStruct((n, d), F32),
        scratch_shapes=[pltpu.VMEM((tm, d), BF16), pltpu.VMEM((tm, d), F32)],
        compiler_params=_cparams(("parallel",)),
        name="conv_ffn",
    )(proj, proj, proj, proj, proj, y_b, conv_w, w_out_a, w_out_b, h, gain, wg, wu, wd, fgain)


def _level_ids(c):
    t = np.arange(c)[:, None]
    s = np.arange(c)[None, :]
    x = np.bitwise_xor(t, s)
    n_levels = int(math.log2(c))
    msb = np.floor(np.log2(np.maximum(x, 1))).astype(np.int64)
    lvl = (n_levels - 1) - msb
    lvl = np.where(x == 0, n_levels, lvl)
    lvl = np.where(s > t, -1, lvl)
    return lvl.astype(np.int32)


def _hgrn2_kernel(q_ref, f_ref, i_ref, g_ref, lb_ref, on_ref, lvl_ref, tri_ref, o_ref,
                  state_ref, b_ref, *, c, n_heads, dk):
    n_levels = int(math.log2(c))
    hc = c // 2

    @pl.when(pl.program_id(1) == 0)
    def _():
        state_ref[...] = jnp.zeros_like(state_ref)

    lvl = lvl_ref[...]
    tri = tri_ref[...]
    row = lax.broadcasted_iota(jnp.int32, (c, dk), 0)
    nt = (((1,), (1,)), ((), ()))
    tn = (((0,), (0,)), ((), ()))

    def head(hd, carry):
        col = pl.multiple_of(hd * dk, dk)
        q_raw = q_ref[0, :, pl.ds(col, dk)].astype(F32)
        f_raw = f_ref[0, :, pl.ds(col, dk)].astype(F32)
        v = i_ref[0, :, pl.ds(col, dk)]
        g_raw = g_ref[0, :, pl.ds(col, dk)].astype(F32)
        lb = lb_ref[:, pl.ds(col, dk)]

        c1 = 0.5 - 0.5 * lb
        c1t = c1 * jnp.tanh(0.5 * f_raw)
        fgate = (lb + c1) + c1t
        k = c1 - c1t
        x = jnp.log2(fgate)
        q = _silu(q_raw)

        x_hi = x.astype(BF16)
        x_lo = (x - x_hi.astype(F32)).astype(BF16)
        b2 = jnp.dot(tri, jnp.concatenate([x_hi, x_lo], axis=1), preferred_element_type=F32)
        b = b2[:, :dk] + b2[:, dk:]
        b_ref[...] = b

        q_bf = q.astype(BF16)
        k_bf = k.astype(BF16)

        def diag_blocks(dense):
            return [dense[r:r + hc, r:r + hc].astype(BF16) for r in (0, hc)]

        plain = diag_blocks(lax.dot_general(q_bf, k_bf, nt, preferred_element_type=F32))
        halves = [jnp.where(lvl == n_levels - 1, blk_, jnp.zeros_like(blk_)) for blk_ in plain]
        cross = None
        for level in range(n_levels):
            blk = c >> level
            half = blk // 2
            if blk >= 8:
                refs = [jnp.broadcast_to(b_ref[pl.ds(m * blk + half - 1, 1), :], (blk, dk))
                        for m in range(c // blk)]
                ref = refs[0] if len(refs) == 1 else jnp.concatenate(refs, axis=0)
                gl = jnp.exp2(-jnp.abs(b - ref))
            elif blk == 4:
                p = row & 3
                f_next = pltpu.roll(fgate, c - 1, 0)
                f_prev = pltpu.roll(fgate, 1, 0)
                gl = jnp.where(p == 0, f_next,
                               jnp.where(p == 1, 1.0,
                                         jnp.where(p == 2, fgate, fgate * f_prev)))
            else:
                gl = jnp.where((row & 1) == 1, fgate, 1.0)
            gl = gl.astype(BF16)
            if level == 0:
                cross = lax.dot_general(q_bf[hc:] * gl[hc:], k_bf[:hc] * gl[:hc], nt,
                                        preferred_element_type=F32).astype(BF16)
            else:
                sl = diag_blocks(lax.dot_general(q_bf * gl, k_bf * gl, nt,
                                                 preferred_element_type=F32))
                halves = [jnp.where(lvl == level - 1, s_, h_) for s_, h_ in zip(sl, halves)]

        o_intra = jnp.concatenate([
            jnp.dot(halves[0], v[:hc], preferred_element_type=F32),
            jnp.dot(jnp.concatenate([cross, halves[1]], axis=1), v, preferred_element_type=F32),
        ], axis=0)

        state = state_ref[hd]
        decay_in = jnp.exp2(b)
        o_inter = jnp.dot((q * decay_in).astype(BF16), state.astype(BF16),
                          preferred_element_type=F32)
        b_last = b_ref[pl.ds(c - 1, 1), :]
        k_out = (k * jnp.exp2(b_last - b)).astype(BF16)
        kv = lax.dot_general(k_out, v, tn, preferred_element_type=F32)
        eye = (lax.broadcasted_iota(jnp.int32, (dk, dk), 0)
               == lax.broadcasted_iota(jnp.int32, (dk, dk), 1))
        scale_col = jnp.sum(jnp.where(eye, jnp.exp2(b_last), 0.0), axis=1, keepdims=True)
        state_ref[hd] = state * scale_col + kv

        o = o_inter + o_intra
        o = o * lax.rsqrt(jnp.mean(o * o, axis=-1, keepdims=True) + RMS_EPS) * on_ref[...]
        o_ref[0, :, pl.ds(col, dk)] = (o * _silu(g_raw)).astype(o_ref.dtype)
        return carry

    lax.fori_loop(0, n_heads, head, 0, unroll=4)


def hgrn2(proj, lower_bound, out_norm, *, n_heads, dk, c):
    bsz, s_len, _ = proj.shape
    w = n_heads * dk
    assert dk == LANES and s_len % c == 0 and (c & (c - 1)) == 0 and c >= 16
    lvl = jnp.asarray(_level_ids(c // 2), BF16)
    tri = jnp.asarray(np.tril(np.ones((c, c), np.float32)), BF16)
    return pl.pallas_call(
        functools.partial(_hgrn2_kernel, c=c, n_heads=n_heads, dk=dk),
        grid=(bsz, s_len // c),
        in_specs=[
            pl.BlockSpec((1, c, w), lambda b, i: (b, i, 0)),
            pl.BlockSpec((1, c, w), lambda b, i: (b, i, 1)),
            pl.BlockSpec((1, c, w), lambda b, i: (b, i, 2)),
            pl.BlockSpec((1, c, w), lambda b, i: (b, i, 3)),
            pl.BlockSpec((1, w), lambda b, i: (0, 0)),
            pl.BlockSpec((1, dk), lambda b, i: (0, 0)),
            pl.BlockSpec((c // 2, c // 2), lambda b, i: (0, 0)),
            pl.BlockSpec((c, c), lambda b, i: (0, 0)),
        ],
        out_specs=pl.BlockSpec((1, c, w), lambda b, i: (b, i, 0)),
        out_shape=jax.ShapeDtypeStruct((bsz, s_len, w), BF16),
        scratch_shapes=[pltpu.VMEM((n_heads, dk, dk), F32), pltpu.VMEM((c, dk), F32)],
        compiler_params=_cparams(("parallel", "arbitrary")),
        name="hgrn2",
    )(proj, proj, proj, proj, lower_bound, out_norm, lvl, tri)


def _tiles(n_tokens, d_ff):
    tm = min(1024, n_tokens)
    tf = 256 if d_ff % 256 == 0 else d_ff
    return tm, tf


def trunk(x, ffn_pre_norm, ffn_pre_w_gate, ffn_pre_w_up, ffn_pre_w_down, mix_norm,
          ffn_post_norm, ffn_post_w_gate, ffn_post_w_up, ffn_post_w_down,
          ab_w_in, ab_conv_w, ab_w_out, c_w_in, c_lower_bounds, c_out_norm, c_w_out,
          final_norm, *, sb_heads, sb_head_dim, c_heads, c_head_dim, sb_tq, c_chunk):
    bsz, s_len, d = x.shape
    depth = ffn_pre_norm.shape[0]
    n = bsz * s_len
    d_ff = ffn_pre_w_gate.shape[-1]
    tm, tf = _tiles(n, d_ff)
    tm_seq = min(tm, s_len)
    bf = lambda t: t.astype(BF16)
    row = lambda t: t.reshape(1, -1).astype(F32)

    lb_soft = jax.nn.softmax(c_lower_bounds.astype(F32), axis=0)
    lb_cum = jnp.cumsum(lb_soft, axis=0)
    lower_bounds = lb_cum - lb_cum[0:1]

    a_width = ab_w_in.shape[-1] // 6
    pre_w = (bf(ffn_pre_w_gate), bf(ffn_pre_w_up), bf(ffn_pre_w_down))
    post_w = (bf(ffn_post_w_gate), bf(ffn_post_w_up), bf(ffn_post_w_down))
    h = x.reshape(n, d)
    for layer in range(depth):
        h = ffn(h, row(ffn_pre_norm[layer]), *pre_w, row(final_norm), layer=layer,
                final_norm=False, tm=tm, tf=tf)
        post = (row(ffn_post_norm[layer]), *post_w, row(final_norm))
        last = layer == depth - 1
        if layer % 2 == 0:
            e = layer // 2
            proj = norm_proj(h, row(mix_norm[layer]), bf(ab_w_in[e]), tm=tm, tn=512)
            y_b = sb_attention(proj.reshape(bsz, s_len, -1), n_heads=sb_heads, dh=sb_head_dim,
                               q_col=3 * a_width, k_col=3 * a_width + sb_heads * sb_head_dim,
                               v_col=3 * a_width + 2 * sb_heads * sb_head_dim, tq=sb_tq)
            w_out = bf(ab_w_out[e])
            h = conv_ffn(proj, y_b.reshape(n, -1), ab_conv_w[e].astype(F32), w_out[:a_width],
                         w_out[a_width:], h, *post, a_width=a_width, s_len=s_len, layer=layer,
                         final_norm=last, tm=tm_seq, tf=tf)
        else:
            o = layer // 2
            proj = norm_proj(h, row(mix_norm[layer]), bf(c_w_in[o]), tm=tm, tn=512)
            y = hgrn2(proj.reshape(bsz, s_len, -1), row(lower_bounds[layer]), row(c_out_norm[o]),
                      n_heads=c_heads, dk=c_head_dim, c=c_chunk)
            h = ffn(h, *post, layer=layer, final_norm=last, tm=tm, tf=tf,
                    mix=(y.reshape(n, -1), bf(c_w_out[o])))
    return h.reshape(bsz, s_len, d)


def kernel(x, ffn_pre_norm, ffn_pre_w_gate, ffn_pre_w_up, ffn_pre_w_down, mix_norm, ffn_post_norm,
           ffn_post_w_gate, ffn_post_w_up, ffn_post_w_down, ab_w_in, ab_conv_w, ab_w_out, c_w_in,
           c_lower_bounds, c_out_norm, c_w_out, final_norm):
    return trunk(x, ffn_pre_norm, ffn_pre_w_gate, ffn_pre_w_up, ffn_pre_w_down, mix_norm,
                 ffn_post_norm, ffn_post_w_gate, ffn_post_w_up, ffn_post_w_down,
                 ab_w_in, ab_conv_w, ab_w_out, c_w_in, c_lower_bounds, c_out_norm, c_w_out,
                 final_norm, sb_heads=8, sb_head_dim=64, c_heads=8, c_head_dim=128,
                 sb_tq=256, c_chunk=256)
```

```python
import functools
import math

import numpy as np
import jax
import jax.numpy as jnp
from jax import lax
from jax.experimental import pallas as pl
from jax.experimental.pallas import tpu as pltpu

RMS_EPS = 1e-6
MACARON_WEIGHT = 0.5
LANES = 128
VMEM_LIMIT_BYTES = 56 * 1024 * 1024
SB_EXIT_LOGDECAY = 104.0
LOG2E = 1.4426950408889634

F32 = jnp.float32
BF16 = jnp.bfloat16


def _cparams(sem):
    return pltpu.CompilerParams(dimension_semantics=sem, vmem_limit_bytes=VMEM_LIMIT_BYTES)


def _resident(shape):
    return pl.BlockSpec(shape, lambda i: (0,) * len(shape), pipeline_mode=pl.Buffered(1))


def _resident_layer(stacked, layer):
    rest = stacked.shape[1:]
    return pl.BlockSpec((None,) + rest, lambda i: (layer,) + (0,) * len(rest),
                        pipeline_mode=pl.Buffered(1))


def _rms_norm_f32(x, gain):
    return x * lax.rsqrt(jnp.mean(x * x, axis=-1, keepdims=True) + RMS_EPS) * gain


def _sigmoid(x):
    return 1.0 / (1.0 + jnp.exp(-x))


def _silu(x):
    hx = 0.5 * x
    return hx + hx * jnp.tanh(hx)


def _mix_ffn_kernel(y_ref, wm_ref, h_ref, gain_ref, wg_ref, wu_ref, wd_ref, fgain_ref, o_ref,
                    xn_ref, acc_ref, *, final_norm, tf):
    o_ref[...] = h_ref[...] + jnp.dot(y_ref[...], wm_ref[...], preferred_element_type=F32)
    _ffn_kernel(o_ref, gain_ref, wg_ref, wu_ref, wd_ref, fgain_ref, o_ref, xn_ref, acc_ref,
                final_norm=final_norm, tf=tf)


def _ffn_kernel(h_ref, gain_ref, wg_ref, wu_ref, wd_ref, fgain_ref, o_ref, xn_ref, acc_ref,
                *, final_norm, tf):
    xn_ref[...] = _rms_norm_f32(h_ref[...], gain_ref[...]).astype(BF16)
    acc_ref[...] = jnp.zeros_like(acc_ref)

    def chunk(j, carry):
        col = pl.multiple_of(j * tf, tf)
        xn = xn_ref[...]
        g = jnp.dot(xn, wg_ref[:, pl.ds(col, tf)], preferred_element_type=F32)
        u = jnp.dot(xn, wu_ref[:, pl.ds(col, tf)], preferred_element_type=F32)
        a = (g * _sigmoid(g) * u).astype(BF16)
        acc_ref[...] += jnp.dot(a, wd_ref[pl.ds(col, tf), :], preferred_element_type=F32)
        return carry

    lax.fori_loop(0, wg_ref.shape[1] // tf, chunk, 0, unroll=True)
    out = h_ref[...] + MACARON_WEIGHT * acc_ref[...]
    if final_norm:
        out = _rms_norm_f32(out, fgain_ref[...])
    o_ref[...] = out


def ffn(h, gain, wg, wu, wd, fgain, *, layer, final_norm, tm, tf, mix=None):
    n, d = h.shape
    tile = lambda width: pl.BlockSpec((tm, width), lambda i: (i, 0))
    in_specs = [tile(d), _resident((1, d)), _resident_layer(wg, layer), _resident_layer(wu, layer),
                _resident_layer(wd, layer), _resident((1, d))]
    args = (h, gain, wg, wu, wd, fgain)
    body = _ffn_kernel
    if mix is not None:
        y, wm = mix
        in_specs = [tile(y.shape[1]), _resident(wm.shape)] + in_specs
        args = (y, wm) + args
        body = _mix_ffn_kernel
    return pl.pallas_call(
        functools.partial(body, final_norm=final_norm, tf=tf),
        grid=(n // tm,),
        in_specs=in_specs,
        out_specs=tile(d),
        out_shape=jax.ShapeDtypeStruct((n, d), F32),
        scratch_shapes=[pltpu.VMEM((tm, d), BF16), pltpu.VMEM((tm, d), F32)],
        compiler_params=_cparams(("parallel",)),
        name="ffn" if mix is None else "mix_ffn",
    )(*args)


def _norm_proj_kernel(h_ref, gain_ref, w_ref, o_ref, xn_ref, *, tn):
    xn_ref[...] = _rms_norm_f32(h_ref[...], gain_ref[...]).astype(BF16)

    def chunk(j, carry):
        col = pl.multiple_of(j * tn, tn)
        o_ref[:, pl.ds(col, tn)] = jnp.dot(xn_ref[...], w_ref[:, pl.ds(col, tn)],
                                           preferred_element_type=F32).astype(o_ref.dtype)
        return carry

    lax.fori_loop(0, w_ref.shape[1] // tn, chunk, 0, unroll=True)


def norm_proj(h, gain, w, *, tm, tn):
    n, d = h.shape
    c = w.shape[1]
    resident = lambda shape: pl.BlockSpec(shape, lambda i: (0, 0), pipeline_mode=pl.Buffered(1))
    return pl.pallas_call(
        functools.partial(_norm_proj_kernel, tn=tn),
        grid=(n // tm,),
        in_specs=[pl.BlockSpec((tm, d), lambda i: (i, 0)), resident((1, d)), resident((d, c))],
        out_specs=pl.BlockSpec((tm, c), lambda i: (i, 0)),
        out_shape=jax.ShapeDtypeStruct((n, c), BF16),
        scratch_shapes=[pltpu.VMEM((tm, d), BF16)],
        compiler_params=_cparams(("parallel",)),
        name="norm_proj",
    )(h, gain, w)


def _sb_attn_kernel(q_ref, k_ref, v_ref, tri_ref, o_ref, acc_ref, dec_ref, *, tq, dh, scale, n_q):
    lane = lax.broadcasted_iota(jnp.int32, (tq, LANES), 1)
    head_b = lane >= dh
    tri = tri_ref[...]
    row = lax.broadcasted_iota(jnp.int32, (tq, tq), 0)
    col = lax.broadcasted_iota(jnp.int32, (tq, tq), 1)
    causal = col < row
    nt = (((1,), (1,)), ((), ()))

    def key_block(j):
        start = pl.multiple_of(j * tq, tq)
        return k_ref[0, pl.ds(start, tq), :], v_ref[0, pl.ds(start, tq), :]

    def head_block(qh, kb, vb, dec, diagonal):
        z = lax.dot_general(qh, kb, nt, preferred_element_type=F32)
        sp = jnp.maximum(z, 0.0) + jnp.log2(1.0 + jnp.exp2(-jnp.abs(z)))
        if diagonal:
            sp = jnp.where(causal, sp, 0.0)
        later = jnp.dot(sp.astype(BF16), tri, preferred_element_type=F32)
        arg = z - later
        total = later[:, 0:1]
        if dec is not None:
            arg = arg - dec
            total = total + dec
        w = jnp.exp2(arg)
        if diagonal:
            w = jnp.where(causal, w, 0.0)
        return jnp.dot(w.astype(BF16), vb, preferred_element_type=F32), total

    def q_heads_of(i):
        qs = pl.multiple_of(i * tq, tq)
        q = (q_ref[0, pl.ds(qs, tq), :].astype(F32) * scale).astype(BF16)
        zero = jnp.zeros_like(q)
        return jnp.where(head_b, zero, q), jnp.where(head_b, q, zero)

    def head_blocks(i, n_static):
        q_heads = q_heads_of(i)
        blocks = [key_block(i - d) for d in range(n_static)]
        outs = []
        for hh in range(2):
            out, dec = head_block(q_heads[hh], *blocks[0], None, True)
            for kb, vb in blocks[1:]:
                more, dec = head_block(q_heads[hh], kb, vb, dec, False)
                out = out + more
            outs.append(out)
            dec_ref[hh] = dec
        acc_ref[...] = jnp.where(head_b, outs[1], outs[0])

    def finish(i, n_static):
        def undecayed():
            return jnp.min(dec_ref[...]) < SB_EXIT_LOGDECAY * LOG2E

        def cond(state):
            j, go = state
            return jnp.logical_and(j >= 0, go)

        def body(state):
            j, _ = state
            q_heads = q_heads_of(i)
            kb, vb = key_block(j)
            more = []
            for hh in range(2):
                out, dec = head_block(q_heads[hh], kb, vb, dec_ref[hh], False)
                dec_ref[hh] = dec
                more.append(out)
            acc_ref[...] += jnp.where(head_b, more[1], more[0])
            return j - 1, undecayed()

        lax.while_loop(cond, body, (i - n_static, undecayed()))
        qs = pl.multiple_of(i * tq, tq)
        o_ref[0, pl.ds(qs, tq), :] = acc_ref[...].astype(o_ref.dtype)

    first = jnp.int32(0)
    head_blocks(first, 1)
    finish(first, 1)

    def rest(i, carry):
        head_blocks(i, 2)
        finish(i, 2)
        return carry

    lax.fori_loop(1, n_q, rest, 0)


def sb_attention(proj, *, n_heads, dh, q_col, k_col, v_col, tq):
    bsz, s_len, _ = proj.shape
    assert 2 * dh == LANES and n_heads % 2 == 0
    assert q_col % LANES == 0 and k_col % LANES == 0 and v_col % LANES == 0
    pairs = n_heads // 2
    qb, kb, vb = q_col // LANES, k_col // LANES, v_col // LANES
    tri = jnp.asarray(np.tril(np.ones((tq, tq), np.float32)), BF16)
    seq = lambda c: pl.BlockSpec((1, s_len, LANES), lambda b, p: (b, 0, c + p))
    return pl.pallas_call(
        functools.partial(_sb_attn_kernel, tq=tq, dh=dh, scale=LOG2E / math.sqrt(dh),
                          n_q=s_len // tq),
        grid=(bsz, pairs),
        in_specs=[seq(qb), seq(kb), seq(vb), pl.BlockSpec((tq, tq), lambda b, p: (0, 0))],
        out_specs=seq(0),
        out_shape=jax.ShapeDtypeStruct((bsz, s_len, pairs * LANES), BF16),
        scratch_shapes=[pltpu.VMEM((tq, LANES), F32), pltpu.VMEM((2, tq, 1), F32)],
        compiler_params=_cparams(("parallel", "parallel")),
        name="sb_attn",
    )(proj, proj, proj, tri)


def _conv_ffn_kernel(ab_ref, ac_ref, ax_ref, pc_ref, px_ref, yb_ref, cw_ref, wa_ref, wb_ref,
                     h_ref, gain_ref, wg_ref, wu_ref, wd_ref, fgain_ref, o_ref, xn_ref, acc_ref,
                     *, final_norm, tf, tiles_per_seq):
    seq_start = (pl.program_id(0) % tiles_per_seq) == 0
    u = ac_ref[...].astype(F32) * ax_ref[...].astype(F32)
    prev = pc_ref[...].astype(F32) * px_ref[...].astype(F32)
    prev = jnp.where(seq_start, 0.0, prev)
    row = lax.broadcasted_iota(jnp.int32, u.shape, 0)
    u1 = pltpu.roll(u, 1, 0)
    u1 = jnp.where(row == 0, prev[7:8, :], u1)
    u2 = pltpu.roll(u, 2, 0)
    u2 = jnp.where(row == 0, prev[6:7, :], jnp.where(row == 1, prev[7:8, :], u2))
    cw = cw_ref[...]
    conv = cw[0:1, :] * u2 + cw[1:2, :] * u1 + cw[2:3, :] * u
    y_a = (ab_ref[...].astype(F32) * conv).astype(BF16)
    o_ref[...] = (h_ref[...]
                  + jnp.dot(y_a, wa_ref[...], preferred_element_type=F32)
                  + jnp.dot(yb_ref[...], wb_ref[...], preferred_element_type=F32))
    _ffn_kernel(o_ref, gain_ref, wg_ref, wu_ref, wd_ref, fgain_ref, o_ref, xn_ref, acc_ref,
                final_norm=final_norm, tf=tf)


def conv_ffn(proj, y_b, conv_w, w_out_a, w_out_b, h, gain, wg, wu, wd, fgain, *, a_width, s_len,
             layer, final_norm, tm, tf):
    n, d = h.shape
    assert s_len % tm == 0 and tm % 8 == 0
    halo = tm // 8
    a_cols = lambda c: pl.BlockSpec((tm, a_width), lambda i: (i, c))
    a_halo = lambda c: pl.BlockSpec((8, a_width), lambda i: (jnp.maximum(i * halo - 1, 0), c))
    return pl.pallas_call(
        functools.partial(_conv_ffn_kernel, final_norm=final_norm, tf=tf,
                          tiles_per_seq=s_len // tm),
        grid=(n // tm,),
        in_specs=[
            a_cols(0), a_cols(1), a_cols(2), a_halo(1), a_halo(2),
            pl.BlockSpec((tm, y_b.shape[1]), lambda i: (i, 0)),
            _resident(conv_w.shape), _resident(w_out_a.shape), _resident(w_out_b.shape),
            pl.BlockSpec((tm, d), lambda i: (i, 0)),
            _resident((1, d)), _resident_layer(wg, layer), _resident_layer(wu, layer),
            _resident_layer(wd, layer), _resident((1, d)),
        ],
        out_specs=pl.BlockSpec((tm, d), lambda i: (i, 0)),
        out_shape=jax.ShapeDtypeStruct((n, d), F32),
        scratch_shapes=[pltpu.VMEM((tm, d), BF16), pltpu.VMEM((tm, d), F32)],
        compiler_params=_cparams(("parallel",)),
        name="conv_ffn",
    )(proj, proj, proj, proj, proj, y_b, conv_w, w_out_a, w_out_b, h, gain, wg, wu, wd, fgain)


def _level_ids(c):
    t = np.arange(c)[:, None]
    s = np.arange(c)[None, :]
    x = np.bitwise_xor(t, s)
    n_levels = int(math.log2(c))
    msb = np.floor(np.log2(np.maximum(x, 1))).astype(np.int64)
    lvl = (n_levels - 1) - msb
    lvl = np.where(x == 0, n_levels, lvl)
    lvl = np.where(s > t, -1, lvl)
    return lvl.astype(np.int32)


def _half_sides(c, dk):
    t = np.arange(c)
    sides = [np.where(t % (c >> level) >= (c >> level) // 2, 1.0, -1.0)
             for level in range(int(math.log2(c)) - 2)]
    return np.broadcast_to(np.stack(sides)[:, :, None], (len(sides), c, dk)).astype(np.float32)


def _hgrn2_kernel(q_ref, f_ref, i_ref, g_ref, lb_ref, on_ref, lvl_ref, tri_ref, side_ref, o_ref,
                  state_ref, b_ref, *, c, n_heads, dk):
    n_levels = int(math.log2(c))
    hc = c // 2

    @pl.when(pl.program_id(1) == 0)
    def _():
        state_ref[...] = jnp.zeros_like(state_ref)

    lvl = lvl_ref[...]
    tri = tri_ref[...]
    row = lax.broadcasted_iota(jnp.int32, (c, dk), 0)
    nt = (((1,), (1,)), ((), ()))
    tn = (((0,), (0,)), ((), ()))

    def head(hd, carry):
        col = pl.multiple_of(hd * dk, dk)
        q_raw = q_ref[0, :, pl.ds(col, dk)].astype(F32)
        f_raw = f_ref[0, :, pl.ds(col, dk)].astype(F32)
        v = i_ref[0, :, pl.ds(col, dk)]
        g_raw = g_ref[0, :, pl.ds(col, dk)].astype(F32)
        lb = lb_ref[:, pl.ds(col, dk)]

        c1 = 0.5 - 0.5 * lb
        c1t = c1 * jnp.tanh(0.5 * f_raw)
        fgate = (lb + c1) + c1t
        k = c1 - c1t
        x = jnp.log2(fgate)
        q = _silu(q_raw)

        x_hi = x.astype(BF16)
        x_lo = (x - x_hi.astype(F32)).astype(BF16)
        b2 = jnp.dot(tri, jnp.concatenate([x_hi, x_lo], axis=1), preferred_element_type=F32)
        b = b2[:, :dk] + b2[:, dk:]
        b_ref[...] = b

        q_bf = q.astype(BF16)
        k_bf = k.astype(BF16)

        def diag_blocks(dense):
            return [dense[r:r + hc, r:r + hc].astype(BF16) for r in (0, hc)]

        plain = diag_blocks(lax.dot_general(q_bf, k_bf, nt, preferred_element_type=F32))
        halves = [jnp.where(lvl == n_levels - 1, blk_, jnp.zeros_like(blk_)) for blk_ in plain]
        cross = None
        for level in range(n_levels):
            blk = c >> level
            half = blk // 2
            if blk >= 8:
                refs = [jnp.broadcast_to(b_ref[pl.ds(m * blk + half - 1, 1), :], (blk, dk))
                        for m in range(c // blk)]
                ref = refs[0] if len(refs) == 1 else jnp.concatenate(refs, axis=0)
                gl = jnp.exp2((b - ref) * side_ref[level])
            elif blk == 4:
                p = row & 3
                f_next = pltpu.roll(fgate, c - 1, 0)
                f_prev = pltpu.roll(fgate, 1, 0)
                gl = jnp.where(p == 0, f_next,
                               jnp.where(p == 1, 1.0,
                                         jnp.where(p == 2, fgate, fgate * f_prev)))
            else:
                gl = jnp.where((row & 1) == 1, fgate, 1.0)
            gl = gl.astype(BF16)
            if level == 0:
                cross = lax.dot_general(q_bf[hc:] * gl[hc:], k_bf[:hc] * gl[:hc], nt,
                                        preferred_element_type=F32).astype(BF16)
            else:
                sl = diag_blocks(lax.dot_general(q_bf * gl, k_bf * gl, nt,
                                                 preferred_element_type=F32))
                halves = [jnp.where(lvl == level - 1, s_, h_) for s_, h_ in zip(sl, halves)]

        o_intra = jnp.concatenate([
            jnp.dot(halves[0], v[:hc], preferred_element_type=F32),
            jnp.dot(jnp.concatenate([cross, halves[1]], axis=1), v, preferred_element_type=F32),
        ], axis=0)

        state = state_ref[hd]
        decay_in = jnp.exp2(b)
        o_inter = jnp.dot((q * decay_in).astype(BF16), state.astype(BF16),
                          preferred_element_type=F32)
        b_last = b_ref[pl.ds(c - 1, 1), :]
        k_out = (k * jnp.exp2(b_last - b)).astype(BF16)
        kv = lax.dot_general(k_out, v, tn, preferred_element_type=F32)
        eye = (lax.broadcasted_iota(jnp.int32, (dk, dk), 0)
               == lax.broadcasted_iota(jnp.int32, (dk, dk), 1))
        scale_col = jnp.sum(jnp.where(eye, jnp.exp2(b_last), 0.0), axis=1, keepdims=True)
        state_ref[hd] = state * scale_col + kv

        o = o_inter + o_intra
        o = o * lax.rsqrt(jnp.mean(o * o, axis=-1, keepdims=True) + RMS_EPS) * on_ref[...]
        o_ref[0, :, pl.ds(col, dk)] = (o * _silu(g_raw)).astype(o_ref.dtype)
        return carry

    lax.fori_loop(0, n_heads, head, 0, unroll=True)


def hgrn2(proj, lower_bound, out_norm, *, n_heads, dk, c):
    bsz, s_len, _ = proj.shape
    w = n_heads * dk
    assert dk == LANES and s_len % c == 0 and (c & (c - 1)) == 0 and c >= 16
    lvl = jnp.asarray(_level_ids(c // 2), BF16)
    tri = jnp.asarray(np.tril(np.ones((c, c), np.float32)), BF16)
    side = jnp.asarray(_half_sides(c, dk))
    return pl.pallas_call(
        functools.partial(_hgrn2_kernel, c=c, n_heads=n_heads, dk=dk),
        grid=(bsz, s_len // c),
        in_specs=[
            pl.BlockSpec((1, c, w), lambda b, i: (b, i, 0)),
            pl.BlockSpec((1, c, w), lambda b, i: (b, i, 1)),
            pl.BlockSpec((1, c, w), lambda b, i: (b, i, 2)),
            pl.BlockSpec((1, c, w), lambda b, i: (b, i, 3)),
            pl.BlockSpec((1, w), lambda b, i: (0, 0)),
            pl.BlockSpec((1, dk), lambda b, i: (0, 0)),
            pl.BlockSpec((c // 2, c // 2), lambda b, i: (0, 0)),
            pl.BlockSpec((c, c), lambda b, i: (0, 0)),
            pl.BlockSpec(side.shape, lambda b, i: (0, 0, 0)),
        ],
        out_specs=pl.BlockSpec((1, c, w), lambda b, i: (b, i, 0)),
        out_shape=jax.ShapeDtypeStruct((bsz, s_len, w), BF16),
        scratch_shapes=[pltpu.VMEM((n_heads, dk, dk), F32), pltpu.VMEM((c, dk), F32)],
        compiler_params=_cparams(("parallel", "arbitrary")),
        name="hgrn2",
    )(proj, proj, proj, proj, lower_bound, out_norm, lvl, tri, side)


def _tiles(n_tokens, d_ff):
    tm = min(1024, n_tokens)
    tf = 256 if d_ff % 256 == 0 else d_ff
    return tm, tf


def trunk(x, ffn_pre_norm, ffn_pre_w_gate, ffn_pre_w_up, ffn_pre_w_down, mix_norm,
          ffn_post_norm, ffn_post_w_gate, ffn_post_w_up, ffn_post_w_down,
          ab_w_in, ab_conv_w, ab_w_out, c_w_in, c_lower_bounds, c_out_norm, c_w_out,
          final_norm, *, sb_heads, sb_head_dim, c_heads, c_head_dim, sb_tq, c_chunk):
    bsz, s_len, d = x.shape
    depth = ffn_pre_norm.shape[0]
    n = bsz * s_len
    d_ff = ffn_pre_w_gate.shape[-1]
    tm, tf = _tiles(n, d_ff)
    tm_seq = min(tm, s_len)
    bf = lambda t: t.astype(BF16)
    row = lambda t: t.reshape(1, -1).astype(F32)

    lb_soft = jax.nn.softmax(c_lower_bounds.astype(F32), axis=0)
    lb_cum = jnp.cumsum(lb_soft, axis=0)
    lower_bounds = lb_cum - lb_cum[0:1]

    a_width = ab_w_in.shape[-1] // 6
    pre_w = (bf(ffn_pre_w_gate), bf(ffn_pre_w_up), bf(ffn_pre_w_down))
    post_w = (bf(ffn_post_w_gate), bf(ffn_post_w_up), bf(ffn_post_w_down))
    h = x.reshape(n, d)
    for layer in range(depth):
        h = ffn(h, row(ffn_pre_norm[layer]), *pre_w, row(final_norm), layer=layer,
                final_norm=False, tm=tm, tf=tf)
        post = (row(ffn_post_norm[layer]), *post_w, row(final_norm))
        last = layer == depth - 1
        if layer % 2 == 0:
            e = layer // 2
            proj = norm_proj(h, row(mix_norm[layer]), bf(ab_w_in[e]), tm=tm, tn=512)
            y_b = sb_attention(proj.reshape(bsz, s_len, -1), n_heads=sb_heads, dh=sb_head_dim,
                               q_col=3 * a_width, k_col=3 * a_width + sb_heads * sb_head_dim,
                               v_col=3 * a_width + 2 * sb_heads * sb_head_dim, tq=sb_tq)
            w_out = bf(ab_w_out[e])
            h = conv_ffn(proj, y_b.reshape(n, -1), ab_conv_w[e].astype(F32), w_out[:a_width],
                         w_out[a_width:], h, *post, a_width=a_width, s_len=s_len, layer=layer,
                         final_norm=last, tm=tm_seq, tf=tf)
        else:
            o = layer // 2
            proj = norm_proj(h, row(mix_norm[layer]), bf(c_w_in[o]), tm=tm, tn=512)
            y = hgrn2(proj.reshape(bsz, s_len, -1), row(lower_bounds[layer]), row(c_out_norm[o]),
                      n_heads=c_heads, dk=c_head_dim, c=c_chunk)
            h = ffn(h, *post, layer=layer, final_norm=last, tm=tm, tf=tf,
                    mix=(y.reshape(n, -1), bf(c_w_out[o])))
    return h.reshape(bsz, s_len, d)


def kernel(x, ffn_pre_norm, ffn_pre_w_gate, ffn_pre_w_up, ffn_pre_w_down, mix_norm, ffn_post_norm,
           ffn_post_w_gate, ffn_post_w_up, ffn_post_w_down, ab_w_in, ab_conv_w, ab_w_out, c_w_in,
           c_lower_bounds, c_out_norm, c_w_out, final_norm):
    return trunk(x, ffn_pre_norm, ffn_pre_w_gate, ffn_pre_w_up, ffn_pre_w_down, mix_norm,
                 ffn_post_norm, ffn_post_w_gate, ffn_post_w_up, ffn_post_w_down,
                 ab_w_in, ab_conv_w, ab_w_out, c_w_in, c_lower_bounds, c_out_norm, c_w_out,
                 final_norm, sb_heads=8, sb_head_dim=64, c_heads=8, c_head_dim=128,
                 sb_tq=256, c_chunk=256)
```

```python
import functools
import math

import numpy as np
import jax
import jax.numpy as jnp
from jax import lax
from jax.experimental import pallas as pl
from jax.experimental.pallas import tpu as pltpu

RMS_EPS = 1e-6
MACARON_WEIGHT = 0.5
LANES = 128
VMEM_LIMIT_BYTES = 56 * 1024 * 1024
SB_EXIT_LOGDECAY = 104.0
LOG2E = 1.4426950408889634

F32 = jnp.float32
BF16 = jnp.bfloat16


def _cparams(sem):
    return pltpu.CompilerParams(dimension_semantics=sem, vmem_limit_bytes=VMEM_LIMIT_BYTES)


def _resident(shape):
    return pl.BlockSpec(shape, lambda i: (0,) * len(shape), pipeline_mode=pl.Buffered(1))


def _resident_layer(stacked, layer):
    rest = stacked.shape[1:]
    return pl.BlockSpec((None,) + rest, lambda i: (layer,) + (0,) * len(rest),
                        pipeline_mode=pl.Buffered(1))


def _rms_norm_f32(x, gain):
    return x * lax.rsqrt(jnp.mean(x * x, axis=-1, keepdims=True) + RMS_EPS) * gain


def _sigmoid(x):
    return 1.0 / (1.0 + jnp.exp(-x))


def _silu(x):
    hx = 0.5 * x
    return hx + hx * jnp.tanh(hx)


def _mix_ffn_kernel(y_ref, wm_ref, h_ref, gain_ref, wg_ref, wu_ref, wd_ref, fgain_ref, o_ref,
                    xn_ref, acc_ref, *, final_norm, tf):
    o_ref[...] = h_ref[...] + jnp.dot(y_ref[...], wm_ref[...], preferred_element_type=F32)
    _ffn_kernel(o_ref, gain_ref, wg_ref, wu_ref, wd_ref, fgain_ref, o_ref, xn_ref, acc_ref,
                final_norm=final_norm, tf=tf)


def _ffn_kernel(h_ref, gain_ref, wg_ref, wu_ref, wd_ref, fgain_ref, o_ref, xn_ref, acc_ref,
                *, final_norm, tf):
    xn_ref[...] = _rms_norm_f32(h_ref[...], gain_ref[...]).astype(BF16)
    acc_ref[...] = jnp.zeros_like(acc_ref)

    def chunk(j, carry):
        col = pl.multiple_of(j * tf, tf)
        xn = xn_ref[...]
        g = jnp.dot(xn, wg_ref[:, pl.ds(col, tf)], preferred_element_type=F32)
        u = jnp.dot(xn, wu_ref[:, pl.ds(col, tf)], preferred_element_type=F32)
        a = (g * _sigmoid(g) * u).astype(BF16)
        acc_ref[...] += jnp.dot(a, wd_ref[pl.ds(col, tf), :], preferred_element_type=F32)
        return carry

    lax.fori_loop(0, wg_ref.shape[1] // tf, chunk, 0, unroll=True)
    out = h_ref[...] + MACARON_WEIGHT * acc_ref[...]
    if final_norm:
        out = _rms_norm_f32(out, fgain_ref[...])
    o_ref[...] = out


def ffn(h, gain, wg, wu, wd, fgain, *, layer, final_norm, tm, tf, mix=None):
    n, d = h.shape
    tile = lambda width: pl.BlockSpec((tm, width), lambda i: (i, 0))
    in_specs = [tile(d), _resident((1, d)), _resident_layer(wg, layer), _resident_layer(wu, layer),
                _resident_layer(wd, layer), _resident((1, d))]
    args = (h, gain, wg, wu, wd, fgain)
    body = _ffn_kernel
    if mix is not None:
        y, wm = mix
        in_specs = [tile(y.shape[1]), _resident(wm.shape)] + in_specs
        args = (y, wm) + args
        body = _mix_ffn_kernel
    return pl.pallas_call(
        functools.partial(body, final_norm=final_norm, tf=tf),
        grid=(n // tm,),
        in_specs=in_specs,
        out_specs=tile(d),
        out_shape=jax.ShapeDtypeStruct((n, d), F32),
        scratch_shapes=[pltpu.VMEM((tm, d), BF16), pltpu.VMEM((tm, d), F32)],
        compiler_params=_cparams(("parallel",)),
        name="ffn" if mix is None else "mix_ffn",
    )(*args)


def _norm_proj_kernel(h_ref, gain_ref, w_ref, o_ref, xn_ref, *, tn):
    xn_ref[...] = _rms_norm_f32(h_ref[...], gain_ref[...]).astype(BF16)

    def chunk(j, carry):
        col = pl.multiple_of(j * tn, tn)
        o_ref[:, pl.ds(col, tn)] = jnp.dot(xn_ref[...], w_ref[:, pl.ds(col, tn)],
                                           preferred_element_type=F32).astype(o_ref.dtype)
        return carry

    lax.fori_loop(0, w_ref.shape[1] // tn, chunk, 0, unroll=True)


def norm_proj(h, gain, w, *, tm, tn):
    n, d = h.shape
    c = w.shape[1]
    resident = lambda shape: pl.BlockSpec(shape, lambda i: (0, 0), pipeline_mode=pl.Buffered(1))
    return pl.pallas_call(
        functools.partial(_norm_proj_kernel, tn=tn),
        grid=(n // tm,),
        in_specs=[pl.BlockSpec((tm, d), lambda i: (i, 0)), resident((1, d)), resident((d, c))],
        out_specs=pl.BlockSpec((tm, c), lambda i: (i, 0)),
        out_shape=jax.ShapeDtypeStruct((n, c), BF16),
        scratch_shapes=[pltpu.VMEM((tm, d), BF16)],
        compiler_params=_cparams(("parallel",)),
        name="norm_proj",
    )(h, gain, w)


def _sb_attn_kernel(q_ref, k_ref, v_ref, tri_ref, o_ref, acc_ref, dec_ref, *, tq, dh, scale, n_q,
                    n_pairs):
    lane = lax.broadcasted_iota(jnp.int32, (tq, LANES), 1)
    head_b = lane >= dh
    tri = tri_ref[...]
    row = lax.broadcasted_iota(jnp.int32, (tq, tq), 0)
    col = lax.broadcasted_iota(jnp.int32, (tq, tq), 1)
    causal = col < row
    nt = (((1,), (1,)), ((), ()))
    heads = [(p, half) for p in range(n_pairs) for half in range(2)]
    pair_lanes = lambda x, p: x[:, p * LANES:(p + 1) * LANES]

    def key_block(j):
        start = pl.multiple_of(j * tq, tq)
        return k_ref[0, pl.ds(start, tq), :], v_ref[0, pl.ds(start, tq), :]

    def head_block(qh, kb, vb, dec, diagonal):
        z = lax.dot_general(qh, kb, nt, preferred_element_type=F32)
        sp = jnp.maximum(z, 0.0) + jnp.log2(1.0 + jnp.exp2(-jnp.abs(z)))
        if diagonal:
            sp = jnp.where(causal, sp, 0.0)
        later = jnp.dot(sp.astype(BF16), tri, preferred_element_type=F32)
        arg = z - later
        total = later[:, 0:1]
        if dec is not None:
            arg = arg - dec
            total = total + dec
        w = jnp.exp2(arg)
        if diagonal:
            w = jnp.where(causal, w, 0.0)
        return jnp.dot(w.astype(BF16), vb, preferred_element_type=F32), total

    def q_heads_of(i):
        qs = pl.multiple_of(i * tq, tq)
        q = (q_ref[0, pl.ds(qs, tq), :].astype(F32) * scale).astype(BF16)
        out = []
        for p, half in heads:
            qp = pair_lanes(q, p)
            zero = jnp.zeros_like(qp)
            out.append(jnp.where(head_b, qp, zero) if half else jnp.where(head_b, zero, qp))
        return out

    def merge(outs):
        pairs = [jnp.where(head_b, outs[2 * p + 1], outs[2 * p]) for p in range(n_pairs)]
        return pairs[0] if n_pairs == 1 else jnp.concatenate(pairs, axis=1)

    def head_blocks(i, n_static):
        q_heads = q_heads_of(i)
        blocks = [key_block(i - d) for d in range(n_static)]
        outs = []
        for h, (p, _) in enumerate(heads):
            out, dec = head_block(q_heads[h], pair_lanes(blocks[0][0], p),
                                  pair_lanes(blocks[0][1], p), None, True)
            for kb, vb in blocks[1:]:
                more, dec = head_block(q_heads[h], pair_lanes(kb, p), pair_lanes(vb, p), dec, False)
                out = out + more
            outs.append(out)
            dec_ref[h] = dec
        acc_ref[...] = merge(outs)

    def finish(i, n_static):
        def undecayed():
            return jnp.min(dec_ref[...]) < SB_EXIT_LOGDECAY * LOG2E

        def cond(state):
            j, go = state
            return jnp.logical_and(j >= 0, go)

        def body(state):
            j, _ = state
            q_heads = q_heads_of(i)
            kb, vb = key_block(j)
            more = []
            for h, (p, _) in enumerate(heads):
                out, dec = head_block(q_heads[h], pair_lanes(kb, p), pair_lanes(vb, p),
                                      dec_ref[h], False)
                dec_ref[h] = dec
                more.append(out)
            acc_ref[...] += merge(more)
            return j - 1, undecayed()

        lax.while_loop(cond, body, (i - n_static, undecayed()))
        qs = pl.multiple_of(i * tq, tq)
        o_ref[0, pl.ds(qs, tq), :] = acc_ref[...].astype(o_ref.dtype)

    first = jnp.int32(0)
    head_blocks(first, 1)
    finish(first, 1)

    def rest(i, carry):
        head_blocks(i, 2)
        finish(i, 2)
        return carry

    lax.fori_loop(1, n_q, rest, 0)


def sb_attention(proj, *, n_heads, dh, q_col, k_col, v_col, tq, pairs_per_step):
    bsz, s_len, _ = proj.shape
    width = pairs_per_step * LANES
    assert 2 * dh == LANES and (n_heads * dh) % width == 0
    assert q_col % width == 0 and k_col % width == 0 and v_col % width == 0
    steps = n_heads * dh // width
    qb, kb, vb = q_col // width, k_col // width, v_col // width
    tri = jnp.asarray(np.tril(np.ones((tq, tq), np.float32)), BF16)
    seq = lambda c: pl.BlockSpec((1, s_len, width), lambda b, p: (b, 0, c + p))
    return pl.pallas_call(
        functools.partial(_sb_attn_kernel, tq=tq, dh=dh, scale=LOG2E / math.sqrt(dh),
                          n_q=s_len // tq, n_pairs=pairs_per_step),
        grid=(bsz, steps),
        in_specs=[seq(qb), seq(kb), seq(vb), pl.BlockSpec((tq, tq), lambda b, p: (0, 0))],
        out_specs=seq(0),
        out_shape=jax.ShapeDtypeStruct((bsz, s_len, n_heads * dh), BF16),
        scratch_shapes=[pltpu.VMEM((tq, width), F32),
                        pltpu.VMEM((2 * pairs_per_step, tq, 1), F32)],
        compiler_params=_cparams(("parallel", "parallel")),
        name="sb_attn",
    )(proj, proj, proj, tri)


def _conv_ffn_kernel(ab_ref, ac_ref, ax_ref, pc_ref, px_ref, yb_ref, cw_ref, wa_ref, wb_ref,
                     h_ref, gain_ref, wg_ref, wu_ref, wd_ref, fgain_ref, o_ref, xn_ref, acc_ref,
                     *, final_norm, tf, tiles_per_seq):
    seq_start = (pl.program_id(0) % tiles_per_seq) == 0
    u = ac_ref[...].astype(F32) * ax_ref[...].astype(F32)
    prev = pc_ref[...].astype(F32) * px_ref[...].astype(F32)
    prev = jnp.where(seq_start, 0.0, prev)
    row = lax.broadcasted_iota(jnp.int32, u.shape, 0)
    u1 = pltpu.roll(u, 1, 0)
    u1 = jnp.where(row == 0, prev[7:8, :], u1)
    u2 = pltpu.roll(u, 2, 0)
    u2 = jnp.where(row == 0, prev[6:7, :], jnp.where(row == 1, prev[7:8, :], u2))
    cw = cw_ref[...]
    conv = cw[0:1, :] * u2 + cw[1:2, :] * u1 + cw[2:3, :] * u
    y_a = (ab_ref[...].astype(F32) * conv).astype(BF16)
    o_ref[...] = (h_ref[...]
                  + jnp.dot(y_a, wa_ref[...], preferred_element_type=F32)
                  + jnp.dot(yb_ref[...], wb_ref[...], preferred_element_type=F32))
    _ffn_kernel(o_ref, gain_ref, wg_ref, wu_ref, wd_ref, fgain_ref, o_ref, xn_ref, acc_ref,
                final_norm=final_norm, tf=tf)


def conv_ffn(proj, y_b, conv_w, w_out_a, w_out_b, h, gain, wg, wu, wd, fgain, *, a_width, s_len,
             layer, final_norm, tm, tf):
    n, d = h.shape
    assert s_len % tm == 0 and tm % 8 == 0
    halo = tm // 8
    a_cols = lambda c: pl.BlockSpec((tm, a_width), lambda i: (i, c))
    a_halo = lambda c: pl.BlockSpec((8, a_width), lambda i: (jnp.maximum(i * halo - 1, 0), c))
    return pl.pallas_call(
        functools.partial(_conv_ffn_kernel, final_norm=final_norm, tf=tf,
                          tiles_per_seq=s_len // tm),
        grid=(n // tm,),
        in_specs=[
            a_cols(0), a_cols(1), a_cols(2), a_halo(1), a_halo(2),
            pl.BlockSpec((tm, y_b.shape[1]), lambda i: (i, 0)),
            _resident(conv_w.shape), _resident(w_out_a.shape), _resident(w_out_b.shape),
            pl.BlockSpec((tm, d), lambda i: (i, 0)),
            _resident((1, d)), _resident_layer(wg, layer), _resident_layer(wu, layer),
            _resident_layer(wd, layer), _resident((1, d)),
        ],
        out_specs=pl.BlockSpec((tm, d), lambda i: (i, 0)),
        out_shape=jax.ShapeDtypeStruct((n, d), F32),
        scratch_shapes=[pltpu.VMEM((tm, d), BF16), pltpu.VMEM((tm, d), F32)],
        compiler_params=_cparams(("parallel",)),
        name="conv_ffn",
    )(proj, proj, proj, proj, proj, y_b, conv_w, w_out_a, w_out_b, h, gain, wg, wu, wd, fgain)


def _level_ids(c):
    t = np.arange(c)[:, None]
    s = np.arange(c)[None, :]
    x = np.bitwise_xor(t, s)
    n_levels = int(math.log2(c))
    msb = np.floor(np.log2(np.maximum(x, 1))).astype(np.int64)
    lvl = (n_levels - 1) - msb
    lvl = np.where(x == 0, n_levels, lvl)
    lvl = np.where(s > t, -1, lvl)
    return lvl.astype(np.int32)


def _half_sides(c, dk):
    t = np.arange(c)
    sides = [np.where(t % (c >> level) >= (c >> level) // 2, 1.0, -1.0)
             for level in range(int(math.log2(c)) - 2)]
    return np.broadcast_to(np.stack(sides)[:, :, None], (len(sides), c, dk)).astype(np.float32)


def _hgrn2_kernel(q_ref, f_ref, i_ref, g_ref, lb_ref, on_ref, lvl_ref, tri_ref, side_ref, o_ref,
                  state_ref, b_ref, *, c, n_heads, dk):
    n_levels = int(math.log2(c))
    hc = c // 2

    @pl.when(pl.program_id(1) == 0)
    def _():
        state_ref[...] = jnp.zeros_like(state_ref)

    lvl = lvl_ref[...]
    tri = tri_ref[...]
    row = lax.broadcasted_iota(jnp.int32, (c, dk), 0)
    nt = (((1,), (1,)), ((), ()))
    tn = (((0,), (0,)), ((), ()))

    def head(hd, carry):
        col = pl.multiple_of(hd * dk, dk)
        q_raw = q_ref[0, :, pl.ds(col, dk)].astype(F32)
        f_raw = f_ref[0, :, pl.ds(col, dk)].astype(F32)
        v = i_ref[0, :, pl.ds(col, dk)]
        g_raw = g_ref[0, :, pl.ds(col, dk)].astype(F32)
        lb = lb_ref[:, pl.ds(col, dk)]

        c1 = 0.5 - 0.5 * lb
        c1t = c1 * jnp.tanh(0.5 * f_raw)
        fgate = (lb + c1) + c1t
        k = c1 - c1t
        x = jnp.log2(fgate)
        q = _silu(q_raw)

        x_hi = x.astype(BF16)
        x_lo = (x - x_hi.astype(F32)).astype(BF16)
        b2 = jnp.dot(tri, jnp.concatenate([x_hi, x_lo], axis=1), preferred_element_type=F32)
        b = b2[:, :dk] + b2[:, dk:]
        b_ref[...] = b

        q_bf = q.astype(BF16)
        k_bf = k.astype(BF16)

        def diag_blocks(dense):
            return [dense[r:r + hc, r:r + hc].astype(BF16) for r in (0, hc)]

        plain = diag_blocks(lax.dot_general(q_bf, k_bf, nt, preferred_element_type=F32))
        halves = [jnp.where(lvl == n_levels - 1, blk_, jnp.zeros_like(blk_)) for blk_ in plain]
        cross = None
        for level in range(n_levels):
            blk = c >> level
            half = blk // 2
            if blk >= 8:
                refs = [jnp.broadcast_to(b_ref[pl.ds(m * blk + half - 1, 1), :], (blk, dk))
                        for m in range(c // blk)]
                ref = refs[0] if len(refs) == 1 else jnp.concatenate(refs, axis=0)
                gl = jnp.exp2((b - ref) * side_ref[level])
            elif blk == 4:
                p = row & 3
                f_next = pltpu.roll(fgate, c - 1, 0)
                f_prev = pltpu.roll(fgate, 1, 0)
                gl = jnp.where(p == 0, f_next,
                               jnp.where(p == 1, 1.0,
                                         jnp.where(p == 2, fgate, fgate * f_prev)))
            else:
                gl = jnp.where((row & 1) == 1, fgate, 1.0)
            gl = gl.astype(BF16)
            if level == 0:
                cross = lax.dot_general(q_bf[hc:] * gl[hc:], k_bf[:hc] * gl[:hc], nt,
                                        preferred_element_type=F32).astype(BF16)
            else:
                sl = diag_blocks(lax.dot_general(q_bf * gl, k_bf * gl, nt,
                                                 preferred_element_type=F32))
                halves = [jnp.where(lvl == level - 1, s_, h_) for s_, h_ in zip(sl, halves)]

        o_intra = jnp.concatenate([
            jnp.dot(halves[0], v[:hc], preferred_element_type=F32),
            jnp.dot(jnp.concatenate([cross, halves[1]], axis=1), v, preferred_element_type=F32),
        ], axis=0)

        state = state_ref[hd]
        decay_in = jnp.exp2(b)
        o_inter = jnp.dot((q * decay_in).astype(BF16), state.astype(BF16),
                          preferred_element_type=F32)
        b_last = b_ref[pl.ds(c - 1, 1), :]
        k_out = (k * jnp.exp2(b_last - b)).astype(BF16)
        kv = lax.dot_general(k_out, v, tn, preferred_element_type=F32)
        eye = (lax.broadcasted_iota(jnp.int32, (dk, dk), 0)
               == lax.broadcasted_iota(jnp.int32, (dk, dk), 1))
        scale_col = jnp.sum(jnp.where(eye, jnp.exp2(b_last), 0.0), axis=1, keepdims=True)
        state_ref[hd] = state * scale_col + kv

        o = o_inter + o_intra
        o = o * lax.rsqrt(jnp.mean(o * o, axis=-1, keepdims=True) + RMS_EPS) * on_ref[...]
        o_ref[0, :, pl.ds(col, dk)] = (o * _silu(g_raw)).astype(o_ref.dtype)
        return carry

    lax.fori_loop(0, n_heads, head, 0, unroll=True)


def hgrn2(proj, lower_bound, out_norm, *, n_heads, dk, c):
    bsz, s_len, _ = proj.shape
    w = n_heads * dk
    assert dk == LANES and s_len % c == 0 and (c & (c - 1)) == 0 and c >= 16
    lvl = jnp.asarray(_level_ids(c // 2), BF16)
    tri = jnp.asarray(np.tril(np.ones((c, c), np.float32)), BF16)
    side = jnp.asarray(_half_sides(c, dk))
    return pl.pallas_call(
        functools.partial(_hgrn2_kernel, c=c, n_heads=n_heads, dk=dk),
        grid=(bsz, s_len // c),
        in_specs=[
            pl.BlockSpec((1, c, w), lambda b, i: (b, i, 0)),
            pl.BlockSpec((1, c, w), lambda b, i: (b, i, 1)),
            pl.BlockSpec((1, c, w), lambda b, i: (b, i, 2)),
            pl.BlockSpec((1, c, w), lambda b, i: (b, i, 3)),
            pl.BlockSpec((1, w), lambda b, i: (0, 0)),
            pl.BlockSpec((1, dk), lambda b, i: (0, 0)),
            pl.BlockSpec((c // 2, c // 2), lambda b, i: (0, 0)),
            pl.BlockSpec((c, c), lambda b, i: (0, 0)),
            pl.BlockSpec(side.shape, lambda b, i: (0, 0, 0)),
        ],
        out_specs=pl.BlockSpec((1, c, w), lambda b, i: (b, i, 0)),
        out_shape=jax.ShapeDtypeStruct((bsz, s_len, w), BF16),
        scratch_shapes=[pltpu.VMEM((n_heads, dk, dk), F32), pltpu.VMEM((c, dk), F32)],
        compiler_params=_cparams(("parallel", "arbitrary")),
        name="hgrn2",
    )(proj, proj, proj, proj, lower_bound, out_norm, lvl, tri, side)


def _tiles(n_tokens, d_ff):
    tm = min(1024, n_tokens)
    tf = 256 if d_ff % 256 == 0 else d_ff
    return tm, tf


def trunk(x, ffn_pre_norm, ffn_pre_w_gate, ffn_pre_w_up, ffn_pre_w_down, mix_norm,
          ffn_post_norm, ffn_post_w_gate, ffn_post_w_up, ffn_post_w_down,
          ab_w_in, ab_conv_w, ab_w_out, c_w_in, c_lower_bounds, c_out_norm, c_w_out,
          final_norm, *, sb_heads, sb_head_dim, c_heads, c_head_dim, sb_tq, sb_pairs, c_chunk):
    bsz, s_len, d = x.shape
    depth = ffn_pre_norm.shape[0]
    n = bsz * s_len
    d_ff = ffn_pre_w_gate.shape[-1]
    tm, tf = _tiles(n, d_ff)
    tm_seq = min(tm, s_len)
    bf = lambda t: t.astype(BF16)
    row = lambda t: t.reshape(1, -1).astype(F32)

    lb_soft = jax.nn.softmax(c_lower_bounds.astype(F32), axis=0)
    lb_cum = jnp.cumsum(lb_soft, axis=0)
    lower_bounds = lb_cum - lb_cum[0:1]

    a_width = ab_w_in.shape[-1] // 6
    pre_w = (bf(ffn_pre_w_gate), bf(ffn_pre_w_up), bf(ffn_pre_w_down))
    post_w = (bf(ffn_post_w_gate), bf(ffn_post_w_up), bf(ffn_post_w_down))
    h = x.reshape(n, d)
    for layer in range(depth):
        h = ffn(h, row(ffn_pre_norm[layer]), *pre_w, row(final_norm), layer=layer,
                final_norm=False, tm=tm, tf=tf)
        post = (row(ffn_post_norm[layer]), *post_w, row(final_norm))
        last = layer == depth - 1
        if layer % 2 == 0:
            e = layer // 2
            proj = norm_proj(h, row(mix_norm[layer]), bf(ab_w_in[e]), tm=tm, tn=512)
            y_b = sb_attention(proj.reshape(bsz, s_len, -1), n_heads=sb_heads, dh=sb_head_dim,
                               q_col=3 * a_width, k_col=3 * a_width + sb_heads * sb_head_dim,
                               v_col=3 * a_width + 2 * sb_heads * sb_head_dim, tq=sb_tq,
                               pairs_per_step=sb_pairs)
            w_out = bf(ab_w_out[e])
            h = conv_ffn(proj, y_b.reshape(n, -1), ab_conv_w[e].astype(F32), w_out[:a_width],
                         w_out[a_width:], h, *post, a_width=a_width, s_len=s_len, layer=layer,
                         final_norm=last, tm=tm_seq, tf=tf)
        else:
            o = layer // 2
            proj = norm_proj(h, row(mix_norm[layer]), bf(c_w_in[o]), tm=tm, tn=512)
            y = hgrn2(proj.reshape(bsz, s_len, -1), row(lower_bounds[layer]), row(c_out_norm[o]),
                      n_heads=c_heads, dk=c_head_dim, c=c_chunk)
            h = ffn(h, *post, layer=layer, final_norm=last, tm=tm, tf=tf,
                    mix=(y.reshape(n, -1), bf(c_w_out[o])))
    return h.reshape(bsz, s_len, d)


def kernel(x, ffn_pre_norm, ffn_pre_w_gate, ffn_pre_w_up, ffn_pre_w_down, mix_norm, ffn_post_norm,
           ffn_post_w_gate, ffn_post_w_up, ffn_post_w_down, ab_w_in, ab_conv_w, ab_w_out, c_w_in,
           c_lower_bounds, c_out_norm, c_w_out, final_norm):
    return trunk(x, ffn_pre_norm, ffn_pre_w_gate, ffn_pre_w_up, ffn_pre_w_down, mix_norm,
                 ffn_post_norm, ffn_post_w_gate, ffn_post_w_up, ffn_post_w_down,
                 ab_w_in, ab_conv_w, ab_w_out, c_w_in, c_lower_bounds, c_out_norm, c_w_out,
                 final_norm, sb_heads=8, sb_head_dim=64, c_heads=8, c_head_dim=128,
                 sb_tq=256, sb_pairs=4, c_chunk=256)
```

```python
import functools
import math

import numpy as np
import jax
import jax.numpy as jnp
from jax import lax
from jax.experimental import pallas as pl
from jax.experimental.pallas import tpu as pltpu

RMS_EPS = 1e-6
MACARON_WEIGHT = 0.5
LANES = 128
VMEM_LIMIT_BYTES = 56 * 1024 * 1024
SB_EXIT_LOGDECAY = 104.0
LOG2E = 1.4426950408889634

F32 = jnp.float32
BF16 = jnp.bfloat16


def _cparams(sem):
    return pltpu.CompilerParams(dimension_semantics=sem, vmem_limit_bytes=VMEM_LIMIT_BYTES)


def _resident(shape):
    return pl.BlockSpec(shape, lambda i: (0,) * len(shape), pipeline_mode=pl.Buffered(1))


def _resident_layer(stacked, layer):
    rest = stacked.shape[1:]
    return pl.BlockSpec((None,) + rest, lambda i: (layer,) + (0,) * len(rest),
                        pipeline_mode=pl.Buffered(1))


def _rms_norm_f32(x, gain):
    return x * lax.rsqrt(jnp.mean(x * x, axis=-1, keepdims=True) + RMS_EPS) * gain


def _sigmoid(x):
    return 1.0 / (1.0 + jnp.exp(-x))


def _silu(x):
    hx = 0.5 * x
    return hx + hx * jnp.tanh(hx)


def _mix_ffn_kernel(y_ref, wm_ref, h_ref, gain_ref, wg_ref, wu_ref, wd_ref, fgain_ref, o_ref,
                    xn_ref, acc_ref, *, final_norm, tf):
    o_ref[...] = h_ref[...] + jnp.dot(y_ref[...], wm_ref[...], preferred_element_type=F32)
    _ffn_kernel(o_ref, gain_ref, wg_ref, wu_ref, wd_ref, fgain_ref, o_ref, xn_ref, acc_ref,
                final_norm=final_norm, tf=tf)


def _ffn_kernel(h_ref, gain_ref, wg_ref, wu_ref, wd_ref, fgain_ref, o_ref, xn_ref, acc_ref,
                *, final_norm, tf):
    xn_ref[...] = _rms_norm_f32(h_ref[...], gain_ref[...]).astype(BF16)
    acc_ref[...] = jnp.zeros_like(acc_ref)

    def chunk(j, carry):
        col = pl.multiple_of(j * tf, tf)
        xn = xn_ref[...]
        g = jnp.dot(xn, wg_ref[:, pl.ds(col, tf)], preferred_element_type=F32)
        u = jnp.dot(xn, wu_ref[:, pl.ds(col, tf)], preferred_element_type=F32)
        a = (g * _sigmoid(g) * u).astype(BF16)
        acc_ref[...] += jnp.dot(a, wd_ref[pl.ds(col, tf), :], preferred_element_type=F32)
        return carry

    lax.fori_loop(0, wg_ref.shape[1] // tf, chunk, 0, unroll=True)
    out = h_ref[...] + MACARON_WEIGHT * acc_ref[...]
    if final_norm:
        out = _rms_norm_f32(out, fgain_ref[...])
    o_ref[...] = out


def ffn(h, gain, wg, wu, wd, fgain, *, layer, final_norm, tm, tf, mix=None):
    n, d = h.shape
    tile = lambda width: pl.BlockSpec((tm, width), lambda i: (i, 0))
    in_specs = [tile(d), _resident((1, d)), _resident_layer(wg, layer), _resident_layer(wu, layer),
                _resident_layer(wd, layer), _resident((1, d))]
    args = (h, gain, wg, wu, wd, fgain)
    body = _ffn_kernel
    if mix is not None:
        y, wm = mix
        in_specs = [tile(y.shape[1]), _resident(wm.shape)] + in_specs
        args = (y, wm) + args
        body = _mix_ffn_kernel
    return pl.pallas_call(
        functools.partial(body, final_norm=final_norm, tf=tf),
        grid=(n // tm,),
        in_specs=in_specs,
        out_specs=tile(d),
        out_shape=jax.ShapeDtypeStruct((n, d), F32),
        scratch_shapes=[pltpu.VMEM((tm, d), BF16), pltpu.VMEM((tm, d), F32)],
        compiler_params=_cparams(("parallel",)),
        name="ffn" if mix is None else "mix_ffn",
    )(*args)


def _norm_proj_kernel(h_ref, gain_ref, w_ref, o_ref, xn_ref, *, tn):
    xn_ref[...] = _rms_norm_f32(h_ref[...], gain_ref[...]).astype(BF16)

    def chunk(j, carry):
        col = pl.multiple_of(j * tn, tn)
        o_ref[:, pl.ds(col, tn)] = jnp.dot(xn_ref[...], w_ref[:, pl.ds(col, tn)],
                                           preferred_element_type=F32).astype(o_ref.dtype)
        return carry

    lax.fori_loop(0, w_ref.shape[1] // tn, chunk, 0, unroll=True)


def norm_proj(h, gain, w, *, tm, tn):
    n, d = h.shape
    c = w.shape[1]
    resident = lambda shape: pl.BlockSpec(shape, lambda i: (0, 0), pipeline_mode=pl.Buffered(1))
    return pl.pallas_call(
        functools.partial(_norm_proj_kernel, tn=tn),
        grid=(n // tm,),
        in_specs=[pl.BlockSpec((tm, d), lambda i: (i, 0)), resident((1, d)), resident((d, c))],
        out_specs=pl.BlockSpec((tm, c), lambda i: (i, 0)),
        out_shape=jax.ShapeDtypeStruct((n, c), BF16),
        scratch_shapes=[pltpu.VMEM((tm, d), BF16)],
        compiler_params=_cparams(("parallel",)),
        name="norm_proj",
    )(h, gain, w)


def _sb_attn_kernel(q_ref, k_ref, v_ref, tri_ref, o_ref, acc_ref, dec_ref, *, tq, dh, scale, n_q,
                    n_pairs):
    lane = lax.broadcasted_iota(jnp.int32, (tq, LANES), 1)
    head_b = lane >= dh
    tri = tri_ref[...]
    row = lax.broadcasted_iota(jnp.int32, (tq, tq), 0)
    col = lax.broadcasted_iota(jnp.int32, (tq, tq), 1)
    causal = col < row
    nt = (((1,), (1,)), ((), ()))
    heads = [(p, half) for p in range(n_pairs) for half in range(2)]
    pair_lanes = lambda x, p: x[:, p * LANES:(p + 1) * LANES]

    def key_block(j):
        start = pl.multiple_of(j * tq, tq)
        return k_ref[0, pl.ds(start, tq), :], v_ref[0, pl.ds(start, tq), :]

    def head_block(qh, kb, vb, dec, diagonal):
        z = lax.dot_general(qh, kb, nt, preferred_element_type=F32)
        sp = jnp.maximum(z, 0.0) + jnp.log2(1.0 + jnp.exp2(-jnp.abs(z)))
        if diagonal:
            sp = jnp.where(causal, sp, 0.0)
        later = jnp.dot(sp.astype(BF16), tri, preferred_element_type=F32)
        arg = z - later
        total = later[:, 0:1]
        if dec is not None:
            arg = arg - dec
            total = total + dec
        w = jnp.exp2(arg)
        if diagonal:
            w = jnp.where(causal, w, 0.0)
        return jnp.dot(w.astype(BF16), vb, preferred_element_type=F32), total

    def q_heads_of(i):
        qs = pl.multiple_of(i * tq, tq)
        q = (q_ref[0, pl.ds(qs, tq), :].astype(F32) * scale).astype(BF16)
        out = []
        for p, half in heads:
            qp = pair_lanes(q, p)
            zero = jnp.zeros_like(qp)
            out.append(jnp.where(head_b, qp, zero) if half else jnp.where(head_b, zero, qp))
        return out

    def merge(outs):
        pairs = [jnp.where(head_b, outs[2 * p + 1], outs[2 * p]) for p in range(n_pairs)]
        return pairs[0] if n_pairs == 1 else jnp.concatenate(pairs, axis=1)

    def head_blocks(i, n_static):
        q_heads = q_heads_of(i)
        blocks = [key_block(i - d) for d in range(n_static)]
        units = [(h, p, d) for d in range(n_static) for h, (p, _) in enumerate(heads)]
        zs = {(h, d): lax.dot_general(q_heads[h], pair_lanes(blocks[d][0], p), nt,
                                      preferred_element_type=F32) for h, p, d in units}
        sps = {}
        for h, p, d in units:
            z = zs[h, d]
            sp = jnp.maximum(z, 0.0) + jnp.log2(1.0 + jnp.exp2(-jnp.abs(z)))
            sps[h, d] = jnp.where(causal, sp, 0.0) if d == 0 else sp
        laters = {(h, d): jnp.dot(sps[h, d].astype(BF16), tri, preferred_element_type=F32)
                  for h, p, d in units}
        ws, decs = {}, {}
        for h, p, d in units:
            arg = zs[h, d] - laters[h, d]
            total = laters[h, d][:, 0:1]
            if d > 0:
                arg = arg - decs[h]
                total = total + decs[h]
            decs[h] = total
            w = jnp.exp2(arg)
            ws[h, d] = (jnp.where(causal, w, 0.0) if d == 0 else w).astype(BF16)
        outs = []
        for h, (p, _) in enumerate(heads):
            out = sum(jnp.dot(ws[h, d], pair_lanes(blocks[d][1], p), preferred_element_type=F32)
                      for d in range(n_static))
            outs.append(out)
            dec_ref[h] = decs[h]
        acc_ref[...] = merge(outs)

    def finish(i, n_static):
        def undecayed():
            return jnp.min(dec_ref[...]) < SB_EXIT_LOGDECAY * LOG2E

        def cond(state):
            j, go = state
            return jnp.logical_and(j >= 0, go)

        def body(state):
            j, _ = state
            q_heads = q_heads_of(i)
            kb, vb = key_block(j)
            more = []
            for h, (p, _) in enumerate(heads):
                out, dec = head_block(q_heads[h], pair_lanes(kb, p), pair_lanes(vb, p),
                                      dec_ref[h], False)
                dec_ref[h] = dec
                more.append(out)
            acc_ref[...] += merge(more)
            return j - 1, undecayed()

        lax.while_loop(cond, body, (i - n_static, undecayed()))
        qs = pl.multiple_of(i * tq, tq)
        o_ref[0, pl.ds(qs, tq), :] = acc_ref[...].astype(o_ref.dtype)

    first = jnp.int32(0)
    head_blocks(first, 1)
    finish(first, 1)

    def rest(i, carry):
        head_blocks(i, 2)
        finish(i, 2)
        return carry

    lax.fori_loop(1, n_q, rest, 0)


def sb_attention(proj, *, n_heads, dh, q_col, k_col, v_col, tq, pairs_per_step):
    bsz, s_len, _ = proj.shape
    width = pairs_per_step * LANES
    assert 2 * dh == LANES and (n_heads * dh) % width == 0
    assert q_col % width == 0 and k_col % width == 0 and v_col % width == 0
    steps = n_heads * dh // width
    qb, kb, vb = q_col // width, k_col // width, v_col // width
    tri = jnp.asarray(np.tril(np.ones((tq, tq), np.float32)), BF16)
    seq = lambda c: pl.BlockSpec((1, s_len, width), lambda b, p: (b, 0, c + p))
    return pl.pallas_call(
        functools.partial(_sb_attn_kernel, tq=tq, dh=dh, scale=LOG2E / math.sqrt(dh),
                          n_q=s_len // tq, n_pairs=pairs_per_step),
        grid=(bsz, steps),
        in_specs=[seq(qb), seq(kb), seq(vb), pl.BlockSpec((tq, tq), lambda b, p: (0, 0))],
        out_specs=seq(0),
        out_shape=jax.ShapeDtypeStruct((bsz, s_len, n_heads * dh), BF16),
        scratch_shapes=[pltpu.VMEM((tq, width), F32),
                        pltpu.VMEM((2 * pairs_per_step, tq, 1), F32)],
        compiler_params=_cparams(("parallel", "parallel")),
        name="sb_attn",
    )(proj, proj, proj, tri)


def _conv_ffn_kernel(ab_ref, ac_ref, ax_ref, pc_ref, px_ref, yb_ref, cw_ref, wa_ref, wb_ref,
                     h_ref, gain_ref, wg_ref, wu_ref, wd_ref, fgain_ref, o_ref, xn_ref, acc_ref,
                     *, final_norm, tf, tiles_per_seq):
    seq_start = (pl.program_id(0) % tiles_per_seq) == 0
    u = ac_ref[...].astype(F32) * ax_ref[...].astype(F32)
    prev = pc_ref[...].astype(F32) * px_ref[...].astype(F32)
    prev = jnp.where(seq_start, 0.0, prev)
    row = lax.broadcasted_iota(jnp.int32, u.shape, 0)
    u1 = pltpu.roll(u, 1, 0)
    u1 = jnp.where(row == 0, prev[7:8, :], u1)
    u2 = pltpu.roll(u, 2, 0)
    u2 = jnp.where(row == 0, prev[6:7, :], jnp.where(row == 1, prev[7:8, :], u2))
    cw = cw_ref[...]
    conv = cw[0:1, :] * u2 + cw[1:2, :] * u1 + cw[2:3, :] * u
    y_a = (ab_ref[...].astype(F32) * conv).astype(BF16)
    o_ref[...] = (h_ref[...]
                  + jnp.dot(y_a, wa_ref[...], preferred_element_type=F32)
                  + jnp.dot(yb_ref[...], wb_ref[...], preferred_element_type=F32))
    _ffn_kernel(o_ref, gain_ref, wg_ref, wu_ref, wd_ref, fgain_ref, o_ref, xn_ref, acc_ref,
                final_norm=final_norm, tf=tf)


def conv_ffn(proj, y_b, conv_w, w_out_a, w_out_b, h, gain, wg, wu, wd, fgain, *, a_width, s_len,
             layer, final_norm, tm, tf):
    n, d = h.shape
    assert s_len % tm == 0 and tm % 8 == 0
    halo = tm // 8
    a_cols = lambda c: pl.BlockSpec((tm, a_width), lambda i: (i, c))
    a_halo = lambda c: pl.BlockSpec((8, a_width), lambda i: (jnp.maximum(i * halo - 1, 0), c))
    return pl.pallas_call(
        functools.partial(_conv_ffn_kernel, final_norm=final_norm, tf=tf,
                          tiles_per_seq=s_len // tm),
        grid=(n // tm,),
        in_specs=[
            a_cols(0), a_cols(1), a_cols(2), a_halo(1), a_halo(2),
            pl.BlockSpec((tm, y_b.shape[1]), lambda i: (i, 0)),
            _resident(conv_w.shape), _resident(w_out_a.shape), _resident(w_out_b.shape),
            pl.BlockSpec((tm, d), lambda i: (i, 0)),
            _resident((1, d)), _resident_layer(wg, layer), _resident_layer(wu, layer),
            _resident_layer(wd, layer), _resident((1, d)),
        ],
        out_specs=pl.BlockSpec((tm, d), lambda i: (i, 0)),
        out_shape=jax.ShapeDtypeStruct((n, d), F32),
        scratch_shapes=[pltpu.VMEM((tm, d), BF16), pltpu.VMEM((tm, d), F32)],
        compiler_params=_cparams(("parallel",)),
        name="conv_ffn",
    )(proj, proj, proj, proj, proj, y_b, conv_w, w_out_a, w_out_b, h, gain, wg, wu, wd, fgain)


def _level_ids(c):
    t = np.arange(c)[:, None]
    s = np.arange(c)[None, :]
    x = np.bitwise_xor(t, s)
    n_levels = int(math.log2(c))
    msb = np.floor(np.log2(np.maximum(x, 1))).astype(np.int64)
    lvl = (n_levels - 1) - msb
    lvl = np.where(x == 0, n_levels, lvl)
    lvl = np.where(s > t, -1, lvl)
    return lvl.astype(np.int32)


def _half_sides(c, dk):
    t = np.arange(c)
    sides = [np.where(t % (c >> level) >= (c >> level) // 2, 1.0, -1.0)
             for level in range(int(math.log2(c)) - 2)]
    return np.broadcast_to(np.stack(sides)[:, :, None], (len(sides), c, dk)).astype(np.float32)


def _hgrn2_kernel(q_ref, f_ref, i_ref, g_ref, lb_ref, on_ref, lvl_ref, tri_ref, side_ref, o_ref,
                  state_ref, b_ref, *, c, n_heads, dk, group_heads):
    n_levels = int(math.log2(c))
    hc = c // 2

    @pl.when(pl.program_id(1) == 0)
    def _():
        state_ref[...] = jnp.zeros_like(state_ref)

    lvl = lvl_ref[...]
    tri = tri_ref[...]
    row = lax.broadcasted_iota(jnp.int32, (c, dk), 0)
    nt = (((1,), (1,)), ((), ()))
    tn = (((0,), (0,)), ((), ()))

    def diag_blocks(dense):
        return [dense[r:r + hc, r:r + hc].astype(BF16) for r in (0, hc)]

    def prepare(hd):
        cols = slice(hd * dk, (hd + 1) * dk)
        q_raw = q_ref[0, :, cols].astype(F32)
        f_raw = f_ref[0, :, cols].astype(F32)
        lb = lb_ref[:, cols]

        c1 = 0.5 - 0.5 * lb
        c1t = c1 * jnp.tanh(0.5 * f_raw)
        fgate = (lb + c1) + c1t
        k = c1 - c1t
        x = jnp.log2(fgate)
        q = _silu(q_raw)

        x_hi = x.astype(BF16)
        x_lo = (x - x_hi.astype(F32)).astype(BF16)
        b2 = jnp.dot(tri, jnp.concatenate([x_hi, x_lo], axis=1), preferred_element_type=F32)
        b_ref[hd] = b2[:, :dk] + b2[:, dk:]

        q_bf = q.astype(BF16)
        k_bf = k.astype(BF16)
        plain = diag_blocks(lax.dot_general(q_bf, k_bf, nt, preferred_element_type=F32))
        halves = [jnp.where(lvl == n_levels - 1, blk_, jnp.zeros_like(blk_)) for blk_ in plain]
        return dict(hd=hd, cols=cols, q=q, k=k, fgate=fgate, q_bf=q_bf, k_bf=k_bf, halves=halves)

    def add_level(ctx, level):
        hd, q_bf, k_bf, fgate = ctx["hd"], ctx["q_bf"], ctx["k_bf"], ctx["fgate"]
        blk = c >> level
        half = blk // 2
        if blk >= 8:
            refs = [jnp.broadcast_to(b_ref[hd, pl.ds(m * blk + half - 1, 1), :], (blk, dk))
                    for m in range(c // blk)]
            ref = refs[0] if len(refs) == 1 else jnp.concatenate(refs, axis=0)
            gl = jnp.exp2((b_ref[hd] - ref) * side_ref[level])
        elif blk == 4:
            p = row & 3
            f_next = pltpu.roll(fgate, c - 1, 0)
            f_prev = pltpu.roll(fgate, 1, 0)
            gl = jnp.where(p == 0, f_next,
                           jnp.where(p == 1, 1.0, jnp.where(p == 2, fgate, fgate * f_prev)))
        else:
            gl = jnp.where((row & 1) == 1, fgate, 1.0)
        gl = gl.astype(BF16)
        if level == 0:
            ctx["cross"] = lax.dot_general(q_bf[hc:] * gl[hc:], k_bf[:hc] * gl[:hc], nt,
                                           preferred_element_type=F32).astype(BF16)
        else:
            sl = diag_blocks(lax.dot_general(q_bf * gl, k_bf * gl, nt, preferred_element_type=F32))
            ctx["halves"] = [jnp.where(lvl == level - 1, s_, h_)
                             for s_, h_ in zip(sl, ctx["halves"])]

    def read_out(ctx):
        hd, cols, q, k, halves = ctx["hd"], ctx["cols"], ctx["q"], ctx["k"], ctx["halves"]
        v = i_ref[0, :, cols]
        b = b_ref[hd]
        o_intra = jnp.concatenate([
            jnp.dot(halves[0], v[:hc], preferred_element_type=F32),
            jnp.dot(jnp.concatenate([ctx["cross"], halves[1]], axis=1), v,
                    preferred_element_type=F32),
        ], axis=0)

        state = state_ref[hd]
        decay_in = jnp.exp2(b)
        o_inter = jnp.dot((q * decay_in).astype(BF16), state.astype(BF16),
                          preferred_element_type=F32)
        b_last = b_ref[hd, pl.ds(c - 1, 1), :]
        k_out = (k * jnp.exp2(b_last - b)).astype(BF16)
        kv = lax.dot_general(k_out, v, tn, preferred_element_type=F32)
        eye = (lax.broadcasted_iota(jnp.int32, (dk, dk), 0)
               == lax.broadcasted_iota(jnp.int32, (dk, dk), 1))
        scale_col = jnp.sum(jnp.where(eye, jnp.exp2(b_last), 0.0), axis=1, keepdims=True)
        state_ref[hd] = state * scale_col + kv

        o = o_inter + o_intra
        o = o * lax.rsqrt(jnp.mean(o * o, axis=-1, keepdims=True) + RMS_EPS) * on_ref[...]
        o_ref[0, :, cols] = (o * _silu(g_ref[0, :, cols].astype(F32))).astype(o_ref.dtype)

    for group in range(0, n_heads, group_heads):
        ctxs = [prepare(hd) for hd in range(group, group + group_heads)]
        for level in range(n_levels):
            for ctx in ctxs:
                add_level(ctx, level)
        for ctx in ctxs:
            read_out(ctx)


def hgrn2(proj, lower_bound, out_norm, *, n_heads, dk, c, group_heads):
    bsz, s_len, _ = proj.shape
    w = n_heads * dk
    assert dk == LANES and s_len % c == 0 and (c & (c - 1)) == 0 and c >= 16
    lvl = jnp.asarray(_level_ids(c // 2), BF16)
    tri = jnp.asarray(np.tril(np.ones((c, c), np.float32)), BF16)
    side = jnp.asarray(_half_sides(c, dk))
    return pl.pallas_call(
        functools.partial(_hgrn2_kernel, c=c, n_heads=n_heads, dk=dk, group_heads=group_heads),
        grid=(bsz, s_len // c),
        in_specs=[
            pl.BlockSpec((1, c, w), lambda b, i: (b, i, 0)),
            pl.BlockSpec((1, c, w), lambda b, i: (b, i, 1)),
            pl.BlockSpec((1, c, w), lambda b, i: (b, i, 2)),
            pl.BlockSpec((1, c, w), lambda b, i: (b, i, 3)),
            pl.BlockSpec((1, w), lambda b, i: (0, 0)),
            pl.BlockSpec((1, dk), lambda b, i: (0, 0)),
            pl.BlockSpec((c // 2, c // 2), lambda b, i: (0, 0)),
            pl.BlockSpec((c, c), lambda b, i: (0, 0)),
            pl.BlockSpec(side.shape, lambda b, i: (0, 0, 0)),
        ],
        out_specs=pl.BlockSpec((1, c, w), lambda b, i: (b, i, 0)),
        out_shape=jax.ShapeDtypeStruct((bsz, s_len, w), BF16),
        scratch_shapes=[pltpu.VMEM((n_heads, dk, dk), F32), pltpu.VMEM((n_heads, c, dk), F32)],
        compiler_params=_cparams(("parallel", "arbitrary")),
        name="hgrn2",
    )(proj, proj, proj, proj, lower_bound, out_norm, lvl, tri, side)


def _tiles(n_tokens, d_ff):
    tm = min(1024, n_tokens)
    tf = 256 if d_ff % 256 == 0 else d_ff
    return tm, tf


def trunk(x, ffn_pre_norm, ffn_pre_w_gate, ffn_pre_w_up, ffn_pre_w_down, mix_norm,
          ffn_post_norm, ffn_post_w_gate, ffn_post_w_up, ffn_post_w_down,
          ab_w_in, ab_conv_w, ab_w_out, c_w_in, c_lower_bounds, c_out_norm, c_w_out,
          final_norm, *, sb_heads, sb_head_dim, c_heads, c_head_dim, sb_tq, sb_pairs, c_chunk, c_group):
    bsz, s_len, d = x.shape
    depth = ffn_pre_norm.shape[0]
    n = bsz * s_len
    d_ff = ffn_pre_w_gate.shape[-1]
    tm, tf = _tiles(n, d_ff)
    tm_seq = min(tm, s_len)
    bf = lambda t: t.astype(BF16)
    row = lambda t: t.reshape(1, -1).astype(F32)

    lb_soft = jax.nn.softmax(c_lower_bounds.astype(F32), axis=0)
    lb_cum = jnp.cumsum(lb_soft, axis=0)
    lower_bounds = lb_cum - lb_cum[0:1]

    a_width = ab_w_in.shape[-1] // 6
    pre_w = (bf(ffn_pre_w_gate), bf(ffn_pre_w_up), bf(ffn_pre_w_down))
    post_w = (bf(ffn_post_w_gate), bf(ffn_post_w_up), bf(ffn_post_w_down))
    h = x.reshape(n, d)
    for layer in range(depth):
        h = ffn(h, row(ffn_pre_norm[layer]), *pre_w, row(final_norm), layer=layer,
                final_norm=False, tm=tm, tf=tf)
        post = (row(ffn_post_norm[layer]), *post_w, row(final_norm))
        last = layer == depth - 1
        if layer % 2 == 0:
            e = layer // 2
            proj = norm_proj(h, row(mix_norm[layer]), bf(ab_w_in[e]), tm=tm, tn=512)
            y_b = sb_attention(proj.reshape(bsz, s_len, -1), n_heads=sb_heads, dh=sb_head_dim,
                               q_col=3 * a_width, k_col=3 * a_width + sb_heads * sb_head_dim,
                               v_col=3 * a_width + 2 * sb_heads * sb_head_dim, tq=sb_tq,
                               pairs_per_step=sb_pairs)
            w_out = bf(ab_w_out[e])
            h = conv_ffn(proj, y_b.reshape(n, -1), ab_conv_w[e].astype(F32), w_out[:a_width],
                         w_out[a_width:], h, *post, a_width=a_width, s_len=s_len, layer=layer,
                         final_norm=last, tm=tm_seq, tf=tf)
        else:
            o = layer // 2
            proj = norm_proj(h, row(mix_norm[layer]), bf(c_w_in[o]), tm=tm, tn=512)
            y = hgrn2(proj.reshape(bsz, s_len, -1), row(lower_bounds[layer]), row(c_out_norm[o]),
                      n_heads=c_heads, dk=c_head_dim, c=c_chunk, group_heads=c_group)
            h = ffn(h, *post, layer=layer, final_norm=last, tm=tm, tf=tf,
                    mix=(y.reshape(n, -1), bf(c_w_out[o])))
    return h.reshape(bsz, s_len, d)


def kernel(x, ffn_pre_norm, ffn_pre_w_gate, ffn_pre_w_up, ffn_pre_w_down, mix_norm, ffn_post_norm,
           ffn_post_w_gate, ffn_post_w_up, ffn_post_w_down, ab_w_in, ab_conv_w, ab_w_out, c_w_in,
           c_lower_bounds, c_out_norm, c_w_out, final_norm):
    return trunk(x, ffn_pre_norm, ffn_pre_w_gate, ffn_pre_w_up, ffn_pre_w_down, mix_norm,
                 ffn_post_norm, ffn_post_w_gate, ffn_post_w_up, ffn_post_w_down,
                 ab_w_in, ab_conv_w, ab_w_out, c_w_in, c_lower_bounds, c_out_norm, c_w_out,
                 final_norm, sb_heads=8, sb_head_dim=64, c_heads=8, c_head_dim=128,
                 sb_tq=256, sb_pairs=4, c_chunk=256, c_group=4)
```
